```python
import math
import jax, jax.numpy as jnp
from jax import lax
import numpy as np

D_MODEL = 1024
BATCH = 32
SEQ = 2048
DEPTH = 2
DEC_BATCH = 2
DEC_SEQ = 16384
PAST_LEN = 128

GRID_W = 64
HEAD_DIM = 64
EPS = 1e-6
NA_HEADS = 4
NA_WIN_ROWS = 8
NA_WIN_COLS = 16
NA_QCOLS = 16
NA_KCOLS = 32
S5_WIDTH = 256
S5_GROUP = 16
S5_GROUPS = S5_WIDTH // S5_GROUP
S5_STATE = 64
S5_DT_MIN = 1e-3
S5_DT_MAX = 1e-1
GQA_Q_HEADS = 4
GQA_KV_HEADS = 2
GQA_BLOCK = 128
ROPE_THETA = 10000.0
ROPE_AXIS_DIM = HEAD_DIM // 2
HGRN_HEADS = 4
HGRN_DK = 64
HGRN_DV = 64
HGRN_CHUNK = 64
MEM_TOKENS = 256
XA_HEADS = 4
XA_HEAD_DIM = 64
XA_W = XA_HEADS * XA_HEAD_DIM
D_FF = 2816
CONV_W = 3

NA_W = NA_HEADS * HEAD_DIM
GQA_W = GQA_Q_HEADS * HEAD_DIM
GQA_KV_W = GQA_KV_HEADS * HEAD_DIM
HGRN_KW = HGRN_HEADS * HGRN_DK
HGRN_W = HGRN_HEADS * HGRN_DV
MIX_W = NA_W + S5_WIDTH + GQA_W + HGRN_W
IN_SPLITS = (NA_W, NA_W, NA_W, S5_WIDTH, GQA_W, GQA_KV_W, GQA_KV_W, HGRN_KW, HGRN_KW, HGRN_KW, HGRN_W, HGRN_W)
IN_W = sum(IN_SPLITS)

kernel_name = "hybrid_bidir_na_s5_gqa_hgrn2_encoder"


def rms_norm(x, w):
    xf = x.astype(jnp.float32)
    y = xf * lax.rsqrt(jnp.mean(xf * xf, axis=-1, keepdims=True) + EPS)
    return y.astype(x.dtype) * w.astype(x.dtype)


def neighbourhood_attention(q, k, v, rpb):
    B, S, H, d = q.shape
    rows = S // GRID_W
    win_r = min(NA_WIN_ROWS, rows)
    qg = q.reshape(B, rows, GRID_W, H, d)
    kg = k.reshape(B, rows, GRID_W, H, d)
    vg = v.reshape(B, rows, GRID_W, H, d)
    r = np.arange(rows)
    r0 = np.clip(r - win_r // 2, 0, rows - win_r)
    row_idx = r0[:, None] + np.arange(win_r)[None, :]
    kr = kg[:, row_idx]
    vr = vg[:, row_idx]
    dr = row_idx - r[:, None] + (NA_WIN_ROWS - 1)
    scale = 1.0 / math.sqrt(d)
    outs = []
    for cb in range(GRID_W // NA_QCOLS):
        qs = cb * NA_QCOLS
        qcols = np.arange(qs, qs + NA_QCOLS)
        ks = int(np.clip(qs - NA_WIN_COLS // 2, 0, GRID_W - NA_KCOLS))
        kcols = np.arange(ks, ks + NA_KCOLS)
        c0 = np.clip(qcols - NA_WIN_COLS // 2, 0, GRID_W - NA_WIN_COLS)
        in_win = (kcols[None, :] >= c0[:, None]) & (kcols[None, :] < c0[:, None] + NA_WIN_COLS)
        dc = np.clip(kcols[None, :] - qcols[:, None], -(NA_WIN_COLS - 1), NA_WIN_COLS - 1) + (NA_WIN_COLS - 1)
        bias = rpb[:, dr[:, None, :, None], dc[None, :, None, :]]
        kb = kr[:, :, :, ks:ks + NA_KCOLS]
        vb = vr[:, :, :, ks:ks + NA_KCOLS]
        s = jnp.einsum('brqhd,brikhd->bhrqik', qg[:, :, qs:qs + NA_QCOLS], kb).astype(jnp.float32) * scale
        s = s + bias.astype(jnp.float32)[None]
        s = jnp.where(in_win[:, None, :], s, -jnp.inf)
        p = jax.nn.softmax(s.reshape(B, H, rows, NA_QCOLS, win_r * NA_KCOLS), axis=-1)
        p = p.reshape(s.shape).astype(v.dtype)
        outs.append(jnp.einsum('bhrqik,brikhd->brqhd', p, vb))
    o = jnp.concatenate(outs, axis=2)
    return o.reshape(B, S, H * d)


def s5_combine(e1, e2):
    a1r, a1i, b1r, b1i = e1
    a2r, a2i, b2r, b2i = e2
    return (a2r * a1r - a2i * a1i,
            a2r * a1i + a2i * a1r,
            a2r * b1r - a2i * b1i + b2r,
            a2r * b1i + a2i * b1r + b2i)


def s5_direction(u, lam_re, lam_im, log_dt, b_re, b_im, c_re, c_im):
    lre = jnp.minimum(lam_re.astype(jnp.float32), -1e-4)
    lim = lam_im.astype(jnp.float32)
    dt = jnp.exp(log_dt.astype(jnp.float32))[:, None]
    mag = jnp.exp(lre * dt)
    abar_re = mag * jnp.cos(lim * dt)
    abar_im = mag * jnp.sin(lim * dt)
    den = lre * lre + lim * lim
    nre = abar_re - 1.0
    nim = abar_im
    coef_re = (nre * lre + nim * lim) / den
    coef_im = (nim * lre - nre * lim) / den
    bre = b_re.astype(jnp.float32)
    bim = b_im.astype(jnp.float32)
    bbar_re = coef_re[..., None] * bre - coef_im[..., None] * bim
    bbar_im = coef_re[..., None] * bim + coef_im[..., None] * bre
    bu_re = jnp.einsum('bsgc,gnc->bsgn', u, bbar_re)
    bu_im = jnp.einsum('bsgc,gnc->bsgn', u, bbar_im)
    a_re = jnp.broadcast_to(abar_re, bu_re.shape)
    a_im = jnp.broadcast_to(abar_im, bu_im.shape)
    _, _, xr, xi = lax.associative_scan(s5_combine, (a_re, a_im, bu_re, bu_im), axis=1)
    return (jnp.einsum('bsgn,gcn->bsgc', xr, c_re.astype(jnp.float32))
            - jnp.einsum('bsgn,gcn->bsgc', xi, c_im.astype(jnp.float32)))


def s5_mixer(u, lam_re, lam_im, log_dt, b_re, b_im, c_re, c_im, d_skip, glu_w, glu_b):
    B, S, _ = u.shape
    uf = u.astype(jnp.float32)
    ug = uf.reshape(B, S, S5_GROUPS, S5_GROUP)
    y_f = s5_direction(ug, lam_re[0], lam_im[0], log_dt[0], b_re[0], b_im[0], c_re[0], c_im[0])
    y_b = jnp.flip(s5_direction(jnp.flip(ug, axis=1), lam_re[1], lam_im[1], log_dt[1],
                                b_re[1], b_im[1], c_re[1], c_im[1]), axis=1)
    y = (y_f + y_b).reshape(B, S, S5_WIDTH) + d_skip.astype(jnp.float32) * uf
    h = jax.nn.gelu(y).astype(u.dtype)
    return h * jax.nn.sigmoid(h @ glu_w + glu_b)


def axial_rope(S, dtype):
    t = jnp.arange(S)
    inv = 1.0 / (ROPE_THETA ** (jnp.arange(0, ROPE_AXIS_DIM, 2, dtype=jnp.float32) / ROPE_AXIS_DIM))
    ang = jnp.concatenate([(t // GRID_W).astype(jnp.float32)[:, None] * inv,
                           (t % GRID_W).astype(jnp.float32)[:, None] * inv], axis=-1)
    return jnp.cos(ang)[:, None, :].astype(dtype), jnp.sin(ang)[:, None, :].astype(dtype)


def apply_rope(x, cos, sin):
    x1 = x[..., :ROPE_AXIS_DIM]
    x2 = x[..., ROPE_AXIS_DIM:]
    return jnp.concatenate([x1 * cos - x2 * sin, x2 * cos + x1 * sin], axis=-1)


def gqa_axial(q, k, v, q_norm_w, k_norm_w):
    B, S, Hq, d = q.shape
    Hkv = k.shape[2]
    grp = Hq // Hkv
    cos, sin = axial_rope(S, q.dtype)
    q = apply_rope(rms_norm(q, q_norm_w), cos, sin)
    k = apply_rope(rms_norm(k, k_norm_w), cos, sin)
    nb = S // GQA_BLOCK
    qb = q.reshape(B, nb, GQA_BLOCK, Hkv, grp, d).transpose(1, 0, 2, 3, 4, 5)
    scale = 1.0 / math.sqrt(d)

    def attend_block(qi):
        s = jnp.einsum('bqhgd,bkhd->bhgqk', qi, k).astype(jnp.float32) * scale
        p = jax.nn.softmax(s, axis=-1).astype(v.dtype)
        return jnp.einsum('bhgqk,bkhd->bqhgd', p, v)

    o = lax.map(attend_block, qb)
    return o.transpose(1, 0, 2, 3, 4, 5).reshape(B, S, Hq * d)


def hgrn2_lower_bounds(lb_param):
    sm = jax.nn.softmax(lb_param.astype(jnp.float32), axis=0)
    return jnp.concatenate([jnp.zeros_like(sm[:1]), jnp.cumsum(sm, axis=0)[:-1]], axis=0)


def hgrn2_scan(q, logf, kin, v):
    B, S, H, dk = q.shape
    dv = v.shape[-1]
    C = HGRN_CHUNK
    nc = S // C

    def to_chunks(a):
        return a.reshape(B, nc, C, H, a.shape[-1]).transpose(1, 0, 3, 2, 4)

    tri = jnp.tril(jnp.ones((C, C), dtype=bool))

    def step(state, xs):
        qc, lfc, kc, vc = xs
        b = jnp.cumsum(lfc, axis=2)
        o_inter = jnp.einsum('bhtk,bhkv->bhtv', qc * jnp.exp(b), state)
        diff = b[:, :, :, None, :] - b[:, :, None, :, :]
        decay = jnp.exp(jnp.where(tri[:, :, None], diff, -jnp.inf))
        att = jnp.einsum('bhtsk,bhsk->bhts', decay * qc[:, :, :, None, :], kc)
        o_intra = jnp.einsum('bhts,bhsv->bhtv', att, vc)
        b_last = b[:, :, -1:, :]
        new_state = (jnp.exp(b_last[:, :, 0, :])[..., None] * state
                     + jnp.einsum('bhsk,bhsv->bhkv', kc * jnp.exp(b_last - b), vc))
        return new_state, o_inter + o_intra

    state0 = jnp.zeros((B, H, dk, dv), dtype=jnp.float32)
    _, o = lax.scan(step, state0, (to_chunks(q), to_chunks(logf), to_chunks(kin), to_chunks(v)))
    return o.transpose(1, 0, 3, 2, 4).reshape(B, S, H, dv)


def hgrn2_bidir(q, zf_fwd, zf_bwd, v, lb):
    B, S, _ = q.shape
    q4 = q.astype(jnp.float32).reshape(B, S, HGRN_HEADS, HGRN_DK)
    v4 = v.astype(jnp.float32).reshape(B, S, HGRN_HEADS, HGRN_DV)
    lbf = lb.astype(jnp.float32).reshape(HGRN_HEADS, HGRN_DK)

    def gates(z):
        z4 = z.astype(jnp.float32).reshape(B, S, HGRN_HEADS, HGRN_DK)
        logf = jnp.logaddexp(jnp.log(lbf), jnp.log1p(-lbf) + jax.nn.log_sigmoid(z4))
        kin = (1.0 - lbf) * jax.nn.sigmoid(-z4)
        return logf, kin

    lf_f, k_f = gates(zf_fwd)
    lf_b, k_b = gates(zf_bwd)
    o_f = hgrn2_scan(q4, lf_f, k_f, v4)
    o_b = jnp.flip(hgrn2_scan(jnp.flip(q4, 1), jnp.flip(lf_b, 1), jnp.flip(k_b, 1), jnp.flip(v4, 1)), 1)
    return (o_f + o_b).reshape(B, S, HGRN_W).astype(q.dtype)


def token_mixer(h, w_in, na_rpb, s5_lambda_re, s5_lambda_im, s5_log_dt, s5_b_re, s5_b_im,
                s5_c_re, s5_c_im, s5_d, s5_glu_w, s5_glu_b, gqa_q_norm_w, gqa_k_norm_w,
                hgrn_lb, mix_out_norm_w, w_out):
    B, S, _ = h.shape
    proj = h @ w_in
    offs = [int(i) for i in np.cumsum(IN_SPLITS)[:-1]]
    qa, ka, va, ub, qc, kc, vc, qd, zf_f, zf_b, vd, gd = jnp.split(proj, offs, axis=-1)
    o_a = neighbourhood_attention(qa.reshape(B, S, NA_HEADS, HEAD_DIM), ka.reshape(B, S, NA_HEADS, HEAD_DIM),
                                  va.reshape(B, S, NA_HEADS, HEAD_DIM), na_rpb)
    o_b = s5_mixer(ub, s5_lambda_re, s5_lambda_im, s5_log_dt, s5_b_re, s5_b_im, s5_c_re, s5_c_im,
                   s5_d, s5_glu_w, s5_glu_b)
    o_c = gqa_axial(qc.reshape(B, S, GQA_Q_HEADS, HEAD_DIM), kc.reshape(B, S, GQA_KV_HEADS, HEAD_DIM),
                    vc.reshape(B, S, GQA_KV_HEADS, HEAD_DIM), gqa_q_norm_w, gqa_k_norm_w)
    o_d = hgrn2_bidir(qd, zf_f, zf_b, vd, hgrn_lb)
    g_a, g_b, g_c, g_d = jnp.split(mix_out_norm_w, [NA_W, NA_W + S5_WIDTH, NA_W + S5_WIDTH + GQA_W])
    merged = jnp.concatenate([rms_norm(o_a, g_a), rms_norm(o_b, g_b), rms_norm(o_c, g_c),
                              rms_norm(o_d, g_d) * jax.nn.silu(gd)], axis=-1)
    return merged @ w_out


def memory_cross_attention(h, mem, norm_mem_w, w_q, w_kv, w_o):
    B, S, _ = h.shape
    M = mem.shape[1]
    mn = rms_norm(mem, norm_mem_w)
    q = (h @ w_q).reshape(B, S, XA_HEADS, XA_HEAD_DIM)
    k, v = jnp.split(mn @ w_kv, 2, axis=-1)
    k = k.reshape(B, M, XA_HEADS, XA_HEAD_DIM)
    v = v.reshape(B, M, XA_HEADS, XA_HEAD_DIM)
    s = jnp.einsum('bshd,bmhd->bhsm', q, k).astype(jnp.float32) / math.sqrt(XA_HEAD_DIM)
    p = jax.nn.softmax(s, axis=-1).astype(v.dtype)
    o = jnp.einsum('bhsm,bmhd->bshd', p, v).reshape(B, S, XA_W)
    return o @ w_o


def conv_glu_ffn(h, w_up, conv_w, conv_b, w_down):
    S = h.shape[1]
    a, g = jnp.split(h @ w_up, 2, axis=-1)
    pad = CONV_W // 2
    gp = jnp.pad(g, ((0, 0), (pad, pad), (0, 0)))
    gc = conv_b + gp[:, 0:S] * conv_w[0]
    for j in range(1, CONV_W):
        gc = gc + gp[:, j:j + S] * conv_w[j]
    return (jax.nn.silu(gc) * a) @ w_down


def encode(x, mem, norm_mix_w, w_in, na_rpb, s5_lambda_re, s5_lambda_im, s5_log_dt, s5_b_re, s5_b_im,
           s5_c_re, s5_c_im, s5_d, s5_glu_w, s5_glu_b, gqa_q_norm_w, gqa_k_norm_w, hgrn_lower_bound,
           mix_out_norm_w, w_out, norm_xattn_w, norm_mem_w, xattn_w_q, xattn_w_kv, xattn_w_o,
           norm_ffn_w, ffn_w_up, ffn_conv_w, ffn_conv_b, ffn_w_down, final_norm_w):
    lb_all = hgrn2_lower_bounds(hgrn_lower_bound)
    for l in range(DEPTH):
        x = x + token_mixer(rms_norm(x, norm_mix_w[l]), w_in[l], na_rpb[l], s5_lambda_re[l], s5_lambda_im[l],
                            s5_log_dt[l], s5_b_re[l], s5_b_im[l], s5_c_re[l], s5_c_im[l], s5_d[l],
                            s5_glu_w[l], s5_glu_b[l], gqa_q_norm_w[l], gqa_k_norm_w[l], lb_all[l],
                            mix_out_norm_w[l], w_out[l])
        x = x + memory_cross_attention(rms_norm(x, norm_xattn_w[l]), mem, norm_mem_w[l],
                                       xattn_w_q[l], xattn_w_kv[l], xattn_w_o[l])
        x = x + conv_glu_ffn(rms_norm(x, norm_ffn_w[l]), ffn_w_up[l], ffn_conv_w[l], ffn_conv_b[l], ffn_w_down[l])
    return rms_norm(x, final_norm_w)


def setup_inputs(seed: int = 0) -> dict:
    key = jax.random.key(seed)
    ks = iter(jax.random.split(key, 48))
    f32 = jnp.float32

    def nrm(shape, scale):
        return jax.random.normal(next(ks), shape, f32) * scale

    def gain(shape):
        return 1.0 + 0.05 * jax.random.normal(next(ks), shape, f32)

    L = DEPTH
    lam_im_base = math.pi * jnp.arange(S5_STATE, dtype=f32)
    return {
        "x_prompt": nrm((BATCH, SEQ, D_MODEL), 1.0),
        "x_sample": nrm((DEC_BATCH, DEC_SEQ, D_MODEL), 1.0),
        "mem_prompt": nrm((BATCH, MEM_TOKENS, D_MODEL), 1.0),
        "mem_sample": nrm((DEC_BATCH, MEM_TOKENS, D_MODEL), 1.0),
        "norm_mix_w": gain((L, D_MODEL)),
        "w_in": nrm((L, D_MODEL, IN_W), D_MODEL ** -0.5),
        "na_rpb": nrm((L, NA_HEADS, 2 * NA_WIN_ROWS - 1, 2 * NA_WIN_COLS - 1), 0.1),
        "s5_lambda_re": -0.5 + nrm((L, 2, S5_GROUPS, S5_STATE), 0.01),
        "s5_lambda_im": lam_im_base + nrm((L, 2, S5_GROUPS, S5_STATE), 0.01),
        "s5_log_dt": jax.random.uniform(next(ks), (L, 2, S5_GROUPS), f32,
                                        math.log(S5_DT_MIN), math.log(S5_DT_MAX)),
        "s5_b_re": nrm((L, 2, S5_GROUPS, S5_STATE, S5_GROUP), (2 * S5_GROUP) ** -0.5),
        "s5_b_im": nrm((L, 2, S5_GROUPS, S5_STATE, S5_GROUP), (2 * S5_GROUP) ** -0.5),
        "s5_c_re": nrm((L, 2, S5_GROUPS, S5_GROUP, S5_STATE), (2 * S5_STATE) ** -0.5),
        "s5_c_im": nrm((L, 2, S5_GROUPS, S5_GROUP, S5_STATE), (2 * S5_STATE) ** -0.5),
        "s5_d": nrm((L, S5_WIDTH), 1.0),
        "s5_glu_w": nrm((L, S5_WIDTH, S5_WIDTH), S5_WIDTH ** -0.5),
        "s5_glu_b": nrm((L, S5_WIDTH), 0.01),
        "gqa_q_norm_w": gain((L, HEAD_DIM)),
        "gqa_k_norm_w": gain((L, HEAD_DIM)),
        "hgrn_lower_bound": nrm((L, HGRN_KW), 0.5),
        "mix_out_norm_w": gain((L, MIX_W)),
        "w_out": nrm((L, MIX_W, D_MODEL), MIX_W ** -0.5),
        "norm_xattn_w": gain((L, D_MODEL)),
        "norm_mem_w": gain((L, D_MODEL)),
        "xattn_w_q": nrm((L, D_MODEL, XA_W), D_MODEL ** -0.5),
        "xattn_w_kv": nrm((L, D_MODEL, 2 * XA_W), D_MODEL ** -0.5),
        "xattn_w_o": nrm((L, XA_W, D_MODEL), XA_W ** -0.5),
        "norm_ffn_w": gain((L, D_MODEL)),
        "ffn_w_up": nrm((L, D_MODEL, 2 * D_FF), D_MODEL ** -0.5),
        "ffn_conv_w": nrm((L, CONV_W, D_FF), CONV_W ** -0.5),
        "ffn_conv_b": nrm((L, D_FF), 0.01),
        "ffn_w_down": nrm((L, D_FF, D_MODEL), D_FF ** -0.5),
        "final_norm_w": gain((D_MODEL,)),
    }


def reference(x_prompt, x_sample, mem_prompt, mem_sample, norm_mix_w, w_in, na_rpb, s5_lambda_re, s5_lambda_im,
              s5_log_dt, s5_b_re, s5_b_im, s5_c_re, s5_c_im, s5_d, s5_glu_w, s5_glu_b, gqa_q_norm_w,
              gqa_k_norm_w, hgrn_lower_bound, mix_out_norm_w, w_out, norm_xattn_w, norm_mem_w, xattn_w_q,
              xattn_w_kv, xattn_w_o, norm_ffn_w, ffn_w_up, ffn_conv_w, ffn_conv_b, ffn_w_down, final_norm_w):
    y_prompt = encode(x_prompt, mem_prompt, norm_mix_w, w_in, na_rpb, s5_lambda_re, s5_lambda_im, s5_log_dt,
                      s5_b_re, s5_b_im, s5_c_re, s5_c_im, s5_d, s5_glu_w, s5_glu_b, gqa_q_norm_w, gqa_k_norm_w,
                      hgrn_lower_bound, mix_out_norm_w, w_out, norm_xattn_w, norm_mem_w, xattn_w_q, xattn_w_kv,
                      xattn_w_o, norm_ffn_w, ffn_w_up, ffn_conv_w, ffn_conv_b, ffn_w_down, final_norm_w)
    y_sample = encode(x_sample, mem_sample, norm_mix_w, w_in, na_rpb, s5_lambda_re, s5_lambda_im, s5_log_dt,
                      s5_b_re, s5_b_im, s5_c_re, s5_c_im, s5_d, s5_glu_w, s5_glu_b, gqa_q_norm_w, gqa_k_norm_w,
                      hgrn_lower_bound, mix_out_norm_w, w_out, norm_xattn_w, norm_mem_w, xattn_w_q, xattn_w_kv,
                      xattn_w_o, norm_ffn_w, ffn_w_up, ffn_conv_w, ffn_conv_b, ffn_w_down, final_norm_w)
    return (y_prompt, y_sample)
```

```python
import functools
import math

import numpy as np
import jax
import jax.numpy as jnp
from jax import lax
from jax.experimental import pallas as pl
from jax.experimental.pallas import tpu as pltpu

D_MODEL = 1024
GRID_W = 64
HEAD_DIM = 64
EPS = 1e-6
NA_HEADS = 4
NA_WIN_ROWS = 8
NA_WIN_COLS = 16
S5_WIDTH = 256
S5_GROUP = 16
S5_GROUPS = 16
S5_STATE = 64
S5_NSTATE = S5_GROUPS * S5_STATE
GQA_Q_HEADS = 4
GQA_KV_HEADS = 2
ROPE_THETA = 10000.0
ROPE_AXIS_DIM = HEAD_DIM // 2
HGRN_HEADS = 4
XA_HEADS = 4
MEM_TOKENS = 256
D_FF = 2816
MIXW = 256
NEG = -1e30

SUBLANES = 8
FF_CHUNK = 256
FF_NCHUNK = D_FF // FF_CHUNK
HALO = 16
HG_CHUNK = 64
HG_LEVELS = (32, 16, 8)
VMEM_LIMIT = 56 * 1024 * 1024

BF = jnp.bfloat16
F32 = jnp.float32


def _cparams(sem):
    return pltpu.CompilerParams(dimension_semantics=sem, vmem_limit_bytes=VMEM_LIMIT)


def _dot(a, b):
    return jnp.dot(a, b, preferred_element_type=F32)


def _dot_nt(a, b):
    return lax.dot_general(a, b, (((1,), (1,)), ((), ())), preferred_element_type=F32)


def _dot_tn(a, b):
    return lax.dot_general(a, b, (((0,), (0,)), ((), ())), preferred_element_type=F32)


def _split3(x):
    h1 = x.astype(BF)
    r1 = x - h1.astype(F32)
    h2 = r1.astype(BF)
    r2 = r1 - h2.astype(F32)
    return h1, h2, r2.astype(BF)


def _rms(x, w):
    ms = jnp.mean(x * x, axis=-1, keepdims=True)
    return x * lax.rsqrt(ms + EPS) * w


def _head_mask(shape, axis, h):
    idx = lax.broadcasted_iota(jnp.int32, shape, axis)
    return (idx >= h * HEAD_DIM) & (idx < (h + 1) * HEAD_DIM)


def _const_spec(shape):
    nd = len(shape)
    return pl.BlockSpec(shape, lambda *_: (0,) * nd)


def _in_proj_kernel(x_ref, nw_ref, wa_ref, wb_ref, wq_ref, wk_ref, wvt_ref, wd_ref, ones_ref,
                    gq_ref, gk_ref, cos_ref, sin_ref,
                    qkva_ref, ub_ref, qc_ref, kc_ref, vct_ref, dd_ref):
    x = x_ref[...]
    h = _rms(x, nw_ref[...]).astype(BF)
    qkva_ref[...] = _dot(h, wa_ref[...]).astype(BF)
    ub_ref[...] = _dot(h, wb_ref[...])
    dd_ref[...] = _dot(h, wd_ref[...])
    vct_ref[...] = _dot_nt(wvt_ref[...], h).astype(BF)

    ones = ones_ref[...]
    cos = cos_ref[...]
    sin = sin_ref[...]

    def norm_rope(y, g, width):
        yy = y * y
        hi = yy.astype(BF)
        lo = (yy - hi.astype(F32)).astype(BF)
        ms = (_dot(hi, ones[:width, :width]) + _dot(lo, ones[:width, :width])) * (1.0 / HEAD_DIM)
        yn = y * lax.rsqrt(ms + EPS) * g
        lane = lax.broadcasted_iota(jnp.int32, yn.shape, 1)
        first_half = (lane % HEAD_DIM) < ROPE_AXIS_DIM
        swapped = jnp.where(first_half,
                            pltpu.roll(yn, width - ROPE_AXIS_DIM, axis=1),
                            pltpu.roll(yn, ROPE_AXIS_DIM, axis=1))
        return yn * cos[:, :width] + swapped * sin[:, :width]

    q = norm_rope(_dot(h, wq_ref[...]), gq_ref[...], 4 * HEAD_DIM)
    qc_ref[...] = (q * (1.0 / math.sqrt(HEAD_DIM))).astype(BF)
    k = norm_rope(_dot(h, wk_ref[...]), gk_ref[...][:, :2 * HEAD_DIM], 2 * HEAD_DIM)
    kc_ref[...] = k.astype(BF)


def _in_proj(x, p, tm):
    B, S, _ = x.shape
    grid = (B, S // tm)
    tok = lambda w: pl.BlockSpec((None, tm, w), lambda b, j: (b, j, 0))
    tab = pl.BlockSpec((tm, 4 * HEAD_DIM), lambda b, j: (j, 0))
    ins = [x, p["norm_mix_w"], p["wa"], p["wb"], p["wq"], p["wk"], p["wvt"], p["wd"], p["ones"],
           p["gq"], p["gk"], p["cos"], p["sin"]]
    in_specs = [tok(D_MODEL)] + [_const_spec(a.shape) for a in ins[1:11]] + [tab, tab]
    out_shape = (
        jax.ShapeDtypeStruct((B, S, 768), BF),
        jax.ShapeDtypeStruct((B, S, 256), F32),
        jax.ShapeDtypeStruct((B, S, 256), BF),
        jax.ShapeDtypeStruct((B, S, 128), BF),
        jax.ShapeDtypeStruct((B, 128, S), BF),
        jax.ShapeDtypeStruct((B, S, 1280), F32),
    )
    out_specs = (tok(768), tok(256), tok(256), tok(128),
                 pl.BlockSpec((None, 128, tm), lambda b, j: (b, 0, j)), tok(1280))
    return pl.pallas_call(
        _in_proj_kernel, grid=grid, in_specs=in_specs, out_specs=out_specs, out_shape=out_shape,
        compiler_params=_cparams(("parallel", "parallel")), name="in_proj")(*ins)


NA_CHUNK_ROWS = 8
NA_CHUNK = NA_CHUNK_ROWS * GRID_W
NA_KEYS = NA_WIN_ROWS * GRID_W


def _na_kernel(q_ref, kp_ref, kc_ref, kn_ref, vp_ref, vc_ref, vn_ref, bias_ref, g_ref, o_ref,
               kbuf, vbuf, *, rows):
    j = pl.program_id(1)
    kbuf[0:NA_CHUNK, :] = kp_ref[...]
    kbuf[NA_CHUNK:2 * NA_CHUNK, :] = kc_ref[...]
    kbuf[2 * NA_CHUNK:3 * NA_CHUNK, :] = kn_ref[...]
    for part, ref in enumerate((vp_ref, vc_ref, vn_ref)):
        v = ref[...]
        for h in range(NA_HEADS):
            vbuf[h, part * NA_CHUNK:(part + 1) * NA_CHUNK, :] = jnp.where(
                _head_mask(v.shape, 1, h), v, jnp.zeros_like(v))
    gain = g_ref[...]

    def row_body(rr, carry):
        r = j * NA_CHUNK_ROWS + rr
        r0 = jnp.clip(r - NA_WIN_ROWS // 2, 0, rows - NA_WIN_ROWS)
        off = pl.multiple_of((r0 - j * NA_CHUNK_ROWS + NA_CHUNK_ROWS) * GRID_W, GRID_W)
        variant = r - r0
        qrow = pl.multiple_of(rr * GRID_W, GRID_W)
        q = q_ref[pl.ds(qrow, GRID_W), :]
        kw = kbuf[pl.ds(off, NA_KEYS), :]
        acc = jnp.zeros((GRID_W, MIXW), F32)
        for h in range(NA_HEADS):
            qm = jnp.where(_head_mask(q.shape, 1, h), q, jnp.zeros_like(q))
            s = _dot_nt(qm, kw) + bias_ref[variant, h]
            m = jnp.max(s, axis=-1, keepdims=True)
            e = jnp.exp(s - m)
            l = jnp.sum(e, axis=-1, keepdims=True)
            p = (e * (1.0 / l)).astype(BF)
            acc = acc + _dot(p, vbuf[h, pl.ds(off, NA_KEYS), :])
        o_ref[pl.ds(qrow, GRID_W), :] = _rms(acc, gain).astype(BF)
        return carry

    lax.fori_loop(0, NA_CHUNK_ROWS, row_body, 0)


def _na_bias_table(rpb):
    qcols = np.arange(GRID_W)
    kcols = np.arange(GRID_W)
    c0 = np.clip(qcols - NA_WIN_COLS // 2, 0, GRID_W - NA_WIN_COLS)
    in_win = (kcols[None, :] >= c0[:, None]) & (kcols[None, :] < c0[:, None] + NA_WIN_COLS)
    dc = np.clip(kcols[None, :] - qcols[:, None], -(NA_WIN_COLS - 1), NA_WIN_COLS - 1) + (NA_WIN_COLS - 1)
    variants = []
    for var in range(NA_WIN_ROWS):
        dr = np.arange(NA_WIN_ROWS) - var + (NA_WIN_ROWS - 1)
        bias = rpb[:, dr[:, None, None], dc[None, :, :]]
        bias = jnp.where(in_win[None, None], bias, NEG)
        variants.append(bias.transpose(0, 2, 1, 3).reshape(NA_HEADS, GRID_W, NA_KEYS))
    return jnp.stack(variants).astype(F32)


def _na(qkva, bias, gain):
    B, S, _ = qkva.shape
    rows = S // GRID_W
    nchunk = S // NA_CHUNK
    blk = lambda col, f: pl.BlockSpec((None, NA_CHUNK, MIXW), lambda b, j: (b, f(j), col))
    prev = lambda j: jnp.maximum(j - 1, 0)
    cur = lambda j: j
    nxt = lambda j: jnp.minimum(j + 1, nchunk - 1)
    in_specs = [blk(0, cur), blk(1, prev), blk(1, cur), blk(1, nxt), blk(2, prev), blk(2, cur), blk(2, nxt),
                _const_spec(bias.shape), _const_spec(gain.shape)]
    return pl.pallas_call(
        functools.partial(_na_kernel, rows=rows), grid=(B, nchunk), in_specs=in_specs,
        out_specs=pl.BlockSpec((None, NA_CHUNK, MIXW), lambda b, j: (b, j, 0)),
        out_shape=jax.ShapeDtypeStruct((B, S, MIXW), BF),
        scratch_shapes=[pltpu.VMEM((3 * NA_CHUNK, MIXW), BF), pltpu.VMEM((NA_HEADS, 3 * NA_CHUNK, MIXW), BF)],
        compiler_params=_cparams(("parallel", "parallel")), name="na")(
            qkva, qkva, qkva, qkva, qkva, qkva, qkva, bias, gain)


S5_LANES = 256


def _s5_kernel(*refs, rev, ntile):
    if rev:
        (u_ref, yf_ref, bbd_ref, cc_ref, tab_ref, d_ref, gw_ref, gb_ref, g_ref, o_ref, z_ref, carry_ref) = refs
    else:
        (u_ref, bbd_ref, cc_ref, tab_ref, o_ref, z_ref, carry_ref) = refs
    j = pl.program_id(1)

    @pl.when(j == 0)
    def _():
        carry_ref[...] = jnp.zeros_like(carry_ref)

    u = u_ref[...]
    z_ref[...] = _dot(u.astype(BF), bbd_ref[...])

    for c0 in range(0, S5_NSTATE, S5_LANES):
        re = slice(c0, c0 + S5_LANES)
        im = slice(S5_NSTATE + c0, S5_NSTATE + c0 + S5_LANES)
        steps = [(tab_ref[2 * i, :, re], tab_ref[2 * i + 1, :, re], 1 << i) for i in range(3)]
        pr, pi = tab_ref[6, :, re], tab_ref[7, :, re]

        def tile_body(i, carry):
            cr, ci = carry
            t = (ntile - 1 - i) if rev else i
            row = pl.multiple_of(t * SUBLANES, SUBLANES)
            zr = z_ref[pl.ds(row, SUBLANES), re]
            zi = z_ref[pl.ds(row, SUBLANES), im]
            for ar, ai, sh in steps:
                amt = SUBLANES - sh if rev else sh
                sr = pltpu.roll(zr, amt, axis=0)
                si = pltpu.roll(zi, amt, axis=0)
                zr, zi = zr + ar * sr - ai * si, zi + ar * si + ai * sr
            zr, zi = zr + pr * cr - pi * ci, zi + pr * ci + pi * cr
            z_ref[pl.ds(row, SUBLANES), re] = zr
            z_ref[pl.ds(row, SUBLANES), im] = zi
            last = 0 if rev else SUBLANES - 1
            return (jnp.broadcast_to(zr[last:last + 1, :], zr.shape),
                    jnp.broadcast_to(zi[last:last + 1, :], zi.shape))

        cr, ci = lax.fori_loop(0, ntile, tile_body, (carry_ref[:, re], carry_ref[:, im]))
        carry_ref[:, re] = cr
        carry_ref[:, im] = ci

    y = _dot(z_ref[...].astype(BF), cc_ref[...])
    if not rev:
        o_ref[...] = y
    else:
        y = y + yf_ref[...] + d_ref[...] * u
        hcur = 0.5 * y * (1.0 + jnp.tanh(math.sqrt(2.0 / math.pi) * (y + 0.044715 * (y * y * y))))
        gate = _dot(hcur.astype(BF), gw_ref[...]) + gb_ref[...]
        out = hcur * (1.0 / (1.0 + jnp.exp(-gate)))
        o_ref[...] = _rms(out, g_ref[...]).astype(BF)


def _s5_tables(lam_re, lam_im, log_dt, b_re, b_im, c_re, c_im, rev):
    lre = jnp.minimum(lam_re.astype(F32), -1e-4)
    lim = lam_im.astype(F32)
    dt = jnp.exp(log_dt.astype(F32))[:, None]
    mag = jnp.exp(lre * dt)
    ar = mag * jnp.cos(lim * dt)
    ai = mag * jnp.sin(lim * dt)
    den = lre * lre + lim * lim
    nre = ar - 1.0
    coef_re = (nre * lre + ai * lim) / den
    coef_im = (ai * lre - nre * lim) / den
    bre, bim = b_re.astype(F32), b_im.astype(F32)
    bbar_re = coef_re[..., None] * bre - coef_im[..., None] * bim
    bbar_im = coef_re[..., None] * bim + coef_im[..., None] * bre
    eye = jnp.eye(S5_GROUPS, dtype=F32)
    bd = lambda m: jnp.einsum('gnc,gh->gchn', m, eye).reshape(S5_WIDTH, S5_NSTATE)
    bbd = jnp.concatenate([bd(bbar_re), bd(bbar_im)], axis=1).astype(BF)
    cd = lambda m: jnp.einsum('gcn,gh->gnhc', m.astype(F32), eye).reshape(S5_NSTATE, S5_WIDTH)
    cc = jnp.concatenate([cd(c_re), -cd(c_im)], axis=0).astype(BF)
    a1r, a1i = ar.reshape(1, -1), ai.reshape(1, -1)
    cmul = lambda xr, xi, yr, yi: (xr * yr - xi * yi, xr * yi + xi * yr)
    a2r, a2i = cmul(a1r, a1i, a1r, a1i)
    a4r, a4i = cmul(a2r, a2i, a2r, a2i)
    pw = [(a1r, a1i)]
    for _ in range(SUBLANES - 1):
        pw.append(cmul(pw[-1][0], pw[-1][1], a1r, a1i))
    pr = jnp.concatenate([x[0] for x in pw], axis=0)
    pi = jnp.concatenate([x[1] for x in pw], axis=0)
    rows = np.arange(SUBLANES)[:, None]
    tabs = []
    for (xr, xi), sh in (((a1r, a1i), 1), ((a2r, a2i), 2), ((a4r, a4i), 4)):
        keep = jnp.asarray(rows >= sh, F32)
        tabs += [keep * xr, keep * xi]
    tabs += [pr, pi]
    tab = jnp.stack(tabs)
    if rev:
        tab = tab[:, ::-1, :]
    return bbd, cc, tab


def _s5(u, tabs_f, tabs_b, d_skip, glu_w, glu_b, gain, L):
    B, S, _ = u.shape
    nch = S // L
    ntile = L // SUBLANES
    scratch = [pltpu.VMEM((L, 2 * S5_NSTATE), F32), pltpu.VMEM((SUBLANES, 2 * S5_NSTATE), F32)]
    fwd_blk = pl.BlockSpec((None, L, S5_WIDTH), lambda b, j: (b, j, 0))
    bwd_blk = pl.BlockSpec((None, L, S5_WIDTH), lambda b, j: (b, nch - 1 - j, 0))
    yf = pl.pallas_call(
        functools.partial(_s5_kernel, rev=False, ntile=ntile), grid=(B, nch),
        in_specs=[fwd_blk] + [_const_spec(a.shape) for a in tabs_f],
        out_specs=fwd_blk, out_shape=jax.ShapeDtypeStruct((B, S, S5_WIDTH), F32),
        scratch_shapes=scratch, compiler_params=_cparams(("parallel", "arbitrary")), name="s5_fwd")(u, *tabs_f)
    rest = (d_skip, glu_w, glu_b, gain)
    return pl.pallas_call(
        functools.partial(_s5_kernel, rev=True, ntile=ntile), grid=(B, nch),
        in_specs=[bwd_blk, bwd_blk] + [_const_spec(a.shape) for a in tabs_b + rest],
        out_specs=bwd_blk, out_shape=jax.ShapeDtypeStruct((B, S, S5_WIDTH), BF),
        scratch_shapes=scratch, compiler_params=_cparams(("parallel", "arbitrary")), name="s5_bwd")(
            u, yf, *tabs_b, *rest)


def _gqa_kernel(q_ref, k_ref, vt_ref, g_ref, o_ref, m_ref, l_ref, acc_ref, *, tk, nkv):
    q = q_ref[...]
    tq = q.shape[0]
    halves = (q[:, :128], q[:, 128:])
    left = lax.broadcasted_iota(jnp.int32, (tq, 128), 1) < HEAD_DIM
    zero = jnp.zeros((tq, 128), BF)
    qs = [jnp.where(left, halves[0], zero), jnp.where(left, zero, halves[0]),
          jnp.where(left, halves[1], zero), jnp.where(left, zero, halves[1])]
    m_ref[...] = jnp.full(m_ref.shape, -jnp.inf, F32)
    l_ref[...] = jnp.zeros(l_ref.shape, F32)
    acc_ref[...] = jnp.zeros(acc_ref.shape, F32)

    def kv_body(jb, carry):
        start = pl.multiple_of(jb * tk, tk)
        k = k_ref[pl.ds(start, tk), :]
        vt = vt_ref[:, pl.ds(start, tk)]
        for i in range(4):
            g = i % 2
            st = _dot_nt(k, qs[i])
            m_old = m_ref[i:i + 1, :]
            m_new = jnp.maximum(m_old, jnp.max(st, axis=0, keepdims=True))
            alpha = jnp.exp(m_old - m_new)
            pt = jnp.exp(st - m_new)
            l_ref[i:i + 1, :] = alpha * l_ref[i:i + 1, :] + jnp.sum(pt, axis=0, keepdims=True)
            m_ref[i:i + 1, :] = m_new
            pv = _dot(vt[g * HEAD_DIM:(g + 1) * HEAD_DIM, :], pt.astype(BF))
            acc_ref[i] = alpha * acc_ref[i] + pv
        return carry

    lax.fori_loop(0, nkv, kv_body, 0)
    outs = [acc_ref[i] * (1.0 / l_ref[i:i + 1, :]) for i in range(4)]
    ot = jnp.concatenate([outs[0], outs[2], outs[1], outs[3]], axis=0)
    o_ref[...] = _rms(ot.T, g_ref[...]).astype(BF)


def _gqa(qc, kc, vct, gain, tq, tk):
    B, S, _ = qc.shape
    return pl.pallas_call(
        functools.partial(_gqa_kernel, tk=tk, nkv=S // tk), grid=(B, S // tq),
        in_specs=[pl.BlockSpec((None, tq, 256), lambda b, j: (b, j, 0)),
                  pl.BlockSpec((None, S, 128), lambda b, j: (b, 0, 0)),
                  pl.BlockSpec((None, 128, S), lambda b, j: (b, 0, 0)),
                  _const_spec(gain.shape)],
        out_specs=pl.BlockSpec((None, tq, MIXW), lambda b, j: (b, j, 0)),
        out_shape=jax.ShapeDtypeStruct((B, S, MIXW), BF),
        scratch_shapes=[pltpu.VMEM((SUBLANES, tq), F32), pltpu.VMEM((SUBLANES, tq), F32),
                        pltpu.VMEM((4, HEAD_DIM, tq), F32)],
        compiler_params=_cparams(("parallel", "parallel")), name="gqa")(qc, kc, vct, gain)


def _hgrn_level_masks(rev):
    t = np.arange(HG_CHUNK)[:, None]
    s = np.arange(HG_CHUNK)[None, :]
    out = []
    for m in HG_LEVELS:
        bt, bs = t // m, s // m
        ok = ((bt % 2 == 0) & (bs == bt + 1)) if rev else ((bt % 2 == 1) & (bs == bt - 1))
        out.append(np.tile(ok.astype(np.float32), (1, HGRN_HEADS)))
    return np.stack(out)


def _hgrn_kernel(*refs, rev, nchunk):
    if rev:
        (q_ref, z_ref, v_ref, gate_ref, of_ref, lb_ref, tri_ref, ones_ref, lmask_ref, g_ref,
         o_ref, st_ref) = refs
    else:
        (q_ref, z_ref, v_ref, lb_ref, tri_ref, ones_ref, lmask_ref, o_ref, st_ref) = refs
    C = HG_CHUNK
    j = pl.program_id(1)

    @pl.when(j == 0)
    def _():
        st_ref[...] = jnp.zeros_like(st_ref)

    log_lb = lb_ref[0:1, :]
    log1m_lb = lb_ref[1:2, :]
    one_m_lb = lb_ref[2:3, :]
    tri = tri_ref[...]
    ones = ones_ref[...]
    row_in_tile = lax.broadcasted_iota(jnp.int32, (C, MIXW), 0) % SUBLANES
    bd_mask = (lax.broadcasted_iota(jnp.int32, (MIXW, MIXW), 0) // HEAD_DIM
               == lax.broadcasted_iota(jnp.int32, (MIXW, MIXW), 1) // HEAD_DIM)
    zero_row = jnp.zeros((1, MIXW), F32)

    def stack_heads(x):
        xb = x.astype(BF)
        return jnp.concatenate(
            [jnp.where(_head_mask(xb.shape, 1, h), xb, jnp.zeros_like(xb)) for h in range(HGRN_HEADS)], axis=0)

    def chunk_body(ci, carry):
        c = (nchunk - 1 - ci) if rev else ci
        rows = pl.ds(pl.multiple_of(c * C, C), C)
        q = q_ref[rows, :]
        z = z_ref[rows, :]
        v = v_ref[rows, :]
        ls = jnp.minimum(z, 0.0) - jnp.log1p(jnp.exp(-jnp.abs(z)))
        bv = log1m_lb + ls
        logf = jnp.maximum(log_lb, bv) + jnp.log1p(jnp.exp(-jnp.abs(log_lb - bv)))
        kin = one_m_lb * jnp.exp(ls - z)
        p1, p2, p3 = _split3(logf)
        b = _dot(tri, p1) + _dot(tri, p2) + _dot(tri, p3)
        edge = b[0:1, :] if rev else b[C - 1:C, :]

        st = st_ref[...]
        o = _dot_nt((q * jnp.exp(b)).astype(BF), st.astype(BF))

        att = jnp.zeros((C, HGRN_HEADS * C), F32)
        for li, m in enumerate(HG_LEVELS):
            nb = C // m
            qrefs, krefs = [], []
            for blk in range(nb):
                if rev:
                    qrow = (blk + 1) * m
                    krow = blk * m
                else:
                    qrow = blk * m - 1
                    krow = (blk + 1) * m - 1
                qr = zero_row if (qrow < 0 or qrow >= C) else b[qrow:qrow + 1, :]
                qrefs.append(jnp.broadcast_to(qr, (m, MIXW)))
                krefs.append(jnp.broadcast_to(b[krow:krow + 1, :], (m, MIXW)))
            bq = jnp.concatenate(qrefs, axis=0)
            bk = jnp.concatenate(krefs, axis=0)
            ql = (q * jnp.exp(b - bq)).astype(BF)
            kl = kin * jnp.exp(bk - b)
            att = att + lmask_ref[li] * _dot_nt(ql, stack_heads(kl))
        o = o + _dot(att.astype(BF), stack_heads(v))

        for dlt in range(SUBLANES):
            if dlt == 0:
                d = q * kin
                vs = v
            else:
                amt = SUBLANES * (C // SUBLANES) - dlt if rev else dlt
                valid = (row_in_tile + dlt <= SUBLANES - 1) if rev else (row_in_tile >= dlt)
                ks = pltpu.roll(kin, amt, axis=0)
                bs = pltpu.roll(b, amt, axis=0)
                vs = pltpu.roll(v, amt, axis=0)
                decay = jnp.exp(jnp.where(valid, b - bs, 0.0))
                d = jnp.where(valid, q * ks * decay, 0.0)
            o = o + _dot(d.astype(BF), ones) * vs

        kb = (kin * jnp.exp(edge - b)).astype(BF)
        upd = _dot_tn(v.astype(BF), kb)
        st_ref[...] = st * jnp.exp(edge) + jnp.where(bd_mask, upd, 0.0)

        if rev:
            tot = o + of_ref[rows, :]
            gate = gate_ref[rows, :]
            silu = gate * (1.0 / (1.0 + jnp.exp(-gate)))
            o_ref[rows, :] = (_rms(tot, g_ref[...]) * silu).astype(BF)
        else:
            o_ref[rows, :] = o
        return carry

    lax.fori_loop(0, nchunk, chunk_body, 0)


def _hgrn(dd, lb_tab, gain, ones, tb):
    B, S, _ = dd.shape
    nblk = S // tb
    nchunk = tb // HG_CHUNK
    tri_f = jnp.asarray(np.tril(np.ones((HG_CHUNK, HG_CHUNK), np.float32)), BF)
    tri_b = jnp.asarray(np.triu(np.ones((HG_CHUNK, HG_CHUNK), np.float32)), BF)
    lm_f = jnp.asarray(_hgrn_level_masks(False))
    lm_b = jnp.asarray(_hgrn_level_masks(True))
    fcol = lambda col: pl.BlockSpec((None, tb, MIXW), lambda b, j: (b, j, col))
    bcol = lambda col: pl.BlockSpec((None, tb, MIXW), lambda b, j: (b, nblk - 1 - j, col))
    scratch = [pltpu.VMEM((MIXW, MIXW), F32)]
    consts_f = (lb_tab, tri_f, ones, lm_f)
    of = pl.pallas_call(
        functools.partial(_hgrn_kernel, rev=False, nchunk=nchunk), grid=(B, nblk),
        in_specs=[fcol(0), fcol(1), fcol(3)] + [_const_spec(a.shape) for a in consts_f],
        out_specs=fcol(0), out_shape=jax.ShapeDtypeStruct((B, S, MIXW), F32),
        scratch_shapes=scratch, compiler_params=_cparams(("parallel", "arbitrary")), name="hgrn_fwd")(
            dd, dd, dd, *consts_f)
    consts_b = (lb_tab, tri_b, ones, lm_b, gain)
    return pl.pallas_call(
        functools.partial(_hgrn_kernel, rev=True, nchunk=nchunk), grid=(B, nblk),
        in_specs=[bcol(0), bcol(2), bcol(3), bcol(4), bcol(0)] + [_const_spec(a.shape) for a in consts_b],
        out_specs=bcol(0), out_shape=jax.ShapeDtypeStruct((B, S, MIXW), BF),
        scratch_shapes=scratch, compiler_params=_cparams(("parallel", "arbitrary")), name="hgrn_bwd")(
            dd, dd, dd, dd, of, *consts_b)


def _mem_kv_kernel(mem_ref, nw_ref, wkv_ref, o_ref):
    o_ref[...] = _dot(_rms(mem_ref[...], nw_ref[...]).astype(BF), wkv_ref[...]).astype(BF)


def _mem_kv(mem, nw, wkv):
    B, M, _ = mem.shape
    return pl.pallas_call(
        _mem_kv_kernel, grid=(B,),
        in_specs=[pl.BlockSpec((None, M, D_MODEL), lambda b: (b, 0, 0)), _const_spec(nw.shape),
                  _const_spec(wkv.shape)],
        out_specs=pl.BlockSpec((None, M, 2 * MIXW), lambda b: (b, 0, 0)),
        out_shape=jax.ShapeDtypeStruct((B, M, 2 * MIXW), BF),
        compiler_params=_cparams(("parallel",)), name="mem_kv")(mem, nw, wkv)


def _post_kernel(x_ref, ma_ref, mb_ref, mc_ref, md_ref, wo_ref, nx_ref, wq_ref, kv_ref, wxo_ref, o_ref):
    x1 = x_ref[...]
    for i, m_ref in enumerate((ma_ref, mb_ref, mc_ref, md_ref)):
        x1 = x1 + _dot(m_ref[...], wo_ref[i * MIXW:(i + 1) * MIXW, :])
    hq = _rms(x1, nx_ref[...]).astype(BF)
    q = _dot(hq, wq_ref[...]).astype(BF)
    kmem = kv_ref[:, :MIXW]
    vmem = kv_ref[:, MIXW:]
    acc = jnp.zeros(q.shape, F32)
    for h in range(XA_HEADS):
        qm = jnp.where(_head_mask(q.shape, 1, h), q, jnp.zeros_like(q))
        s = _dot_nt(qm, kmem)
        m = jnp.max(s, axis=-1, keepdims=True)
        e = jnp.exp(s - m)
        p = (e * (1.0 / jnp.sum(e, axis=-1, keepdims=True))).astype(BF)
        vm = jnp.where(_head_mask(vmem.shape, 1, h), vmem, jnp.zeros_like(vmem))
        acc = acc + _dot(p, vm)
    o_ref[...] = x1 + _dot(acc.astype(BF), wxo_ref[...])


def _post(x, ma, mb, mc, md, kv, p, tm):
    B, S, _ = x.shape
    tok = lambda w: pl.BlockSpec((None, tm, w), lambda b, j: (b, j, 0))
    consts = (p["w_out"], p["norm_xattn_w"], p["xq"])
    return pl.pallas_call(
        _post_kernel, grid=(B, S // tm),
        in_specs=[tok(D_MODEL), tok(MIXW), tok(MIXW), tok(MIXW), tok(MIXW)]
        + [_const_spec(a.shape) for a in consts]
        + [pl.BlockSpec((None, MEM_TOKENS, 2 * MIXW), lambda b, j: (b, 0, 0)), _const_spec(p["xo"].shape)],
        out_specs=tok(D_MODEL), out_shape=jax.ShapeDtypeStruct(x.shape, F32),
        compiler_params=_cparams(("parallel", "parallel")), name="post")(
            x, ma, mb, mc, md, *consts, kv, p["xo"])


def _ffn_kernel(xp_ref, xc_ref, xn_ref, nw_ref, wa_ref, wg_ref, cw_ref, cb_ref, wd_ref, fw_ref, o_ref,
                hn_ref, gx_ref, acc_ref, *, tm, final):
    j = pl.program_id(1)
    nj = pl.num_programs(1)
    nw = nw_ref[...]
    hp = _rms(xp_ref[...], nw)
    hnx = _rms(xn_ref[...], nw)
    hn_ref[0:HALO, :] = jnp.where(j > 0, hp, 0.0).astype(BF)
    hn_ref[HALO:HALO + tm, :] = _rms(xc_ref[...], nw).astype(BF)
    hn_ref[HALO + tm:2 * HALO + tm, :] = jnp.where(j < nj - 1, hnx, 0.0).astype(BF)
    acc_ref[...] = jnp.zeros_like(acc_ref)

    def chunk_body(c, carry):
        a = _dot(hn_ref[HALO:HALO + tm, :], wa_ref[c])
        gx_ref[...] = _dot(hn_ref[...], wg_ref[c])
        cw = cw_ref[c]
        gc = (cb_ref[c] + gx_ref[HALO - 1:HALO - 1 + tm, :] * cw[0:1, :]
              + gx_ref[HALO:HALO + tm, :] * cw[1:2, :]
              + gx_ref[HALO + 1:HALO + 1 + tm, :] * cw[2:3, :])
        act = gc * (1.0 / (1.0 + jnp.exp(-gc))) * a
        acc_ref[...] += _dot(act.astype(BF), wd_ref[c])
        return carry

    lax.fori_loop(0, FF_NCHUNK, chunk_body, 0)
    y = xc_ref[...] + acc_ref[...]
    if final:
        y = _rms(y, fw_ref[...])
    o_ref[...] = y


def _ffn(x, p, final_w, tm, final):
    B, S, _ = x.shape
    per = tm // HALO
    nh = S // HALO
    consts = (p["norm_ffn_w"], p["up_a"], p["up_g"], p["conv_w"], p["conv_b"], p["down"], final_w)
    return pl.pallas_call(
        functools.partial(_ffn_kernel, tm=tm, final=final), grid=(B, S // tm),
        in_specs=[pl.BlockSpec((None, HALO, D_MODEL), lambda b, j: (b, jnp.maximum(j * per - 1, 0), 0)),
                  pl.BlockSpec((None, tm, D_MODEL), lambda b, j: (b, j, 0)),
                  pl.BlockSpec((None, HALO, D_MODEL), lambda b, j: (b, jnp.minimum((j + 1) * per, nh - 1), 0))]
        + [_const_spec(a.shape) for a in consts],
        out_specs=pl.BlockSpec((None, tm, D_MODEL), lambda b, j: (b, j, 0)),
        out_shape=jax.ShapeDtypeStruct(x.shape, F32),
        scratch_shapes=[pltpu.VMEM((tm + 2 * HALO, D_MODEL), BF), pltpu.VMEM((tm + 2 * HALO, FF_CHUNK), F32),
                        pltpu.VMEM((tm, D_MODEL), F32)],
        compiler_params=_cparams(("parallel", "parallel")), name="ffn")(x, x, x, *consts)


def _rope_tables(S):
    t = jnp.arange(S)
    inv = 1.0 / (ROPE_THETA ** (jnp.arange(0, ROPE_AXIS_DIM, 2, dtype=F32) / ROPE_AXIS_DIM))
    ang = jnp.concatenate([(t // GRID_W).astype(F32)[:, None] * inv,
                           (t % GRID_W).astype(F32)[:, None] * inv], axis=-1)
    cos, sin = jnp.cos(ang), jnp.sin(ang)
    cos_h = jnp.concatenate([cos, cos], axis=-1)
    sin_h = jnp.concatenate([-sin, sin], axis=-1)
    return jnp.tile(cos_h, (1, GQA_Q_HEADS)), jnp.tile(sin_h, (1, GQA_Q_HEADS))


def _layer_params(l, w):
    row = lambda a: a.reshape(1, -1).astype(F32)
    w_in = w["w_in"][l]
    qscale = 1.0 / math.sqrt(HEAD_DIM)
    wa = jnp.concatenate([w_in[:, :256] * qscale, w_in[:, 256:768]], axis=1)
    wq = w_in[:, 1024:1280].reshape(D_MODEL, GQA_Q_HEADS, HEAD_DIM)[:, np.array([0, 2, 1, 3]), :].reshape(
        D_MODEL, 256)
    blk = np.arange(256) // HEAD_DIM
    p = {
        "norm_mix_w": row(w["norm_mix_w"][l]),
        "wa": wa.astype(BF),
        "wb": w_in[:, 768:1024].astype(BF),
        "wq": wq.astype(BF),
        "wk": w_in[:, 1280:1408].astype(BF),
        "wvt": w_in[:, 1408:1536].T.astype(BF),
        "wd": w_in[:, 1536:].astype(BF),
        "ones": jnp.asarray(blk[:, None] == blk[None, :], BF),
        "gq": row(jnp.tile(w["gqa_q_norm_w"][l], GQA_Q_HEADS)),
        "gk": row(jnp.tile(w["gqa_k_norm_w"][l], GQA_Q_HEADS)),
        "na_bias": _na_bias_table(w["na_rpb"][l].astype(F32)),
        "s5_f": _s5_tables(*[w[k][l, 0] for k in ("s5_lambda_re", "s5_lambda_im", "s5_log_dt", "s5_b_re",
                                                    "s5_b_im", "s5_c_re", "s5_c_im")], rev=False),
        "s5_b": _s5_tables(*[w[k][l, 1] for k in ("s5_lambda_re", "s5_lambda_im", "s5_log_dt", "s5_b_re",
                                                    "s5_b_im", "s5_c_re", "s5_c_im")], rev=True),
        "s5_d": row(w["s5_d"][l]),
        "s5_glu_w": w["s5_glu_w"][l].astype(BF),
        "s5_glu_b": row(w["s5_glu_b"][l]),
        "w_out": w["w_out"][l].astype(BF),
        "norm_xattn_w": row(w["norm_xattn_w"][l]),
        "norm_mem_w": row(w["norm_mem_w"][l]),
        "xq": (w["xattn_w_q"][l] * qscale).astype(BF),
        "xkv": w["xattn_w_kv"][l].astype(BF),
        "xo": w["xattn_w_o"][l].astype(BF),
        "norm_ffn_w": row(w["norm_ffn_w"][l]),
        "up_a": w["ffn_w_up"][l][:, :D_FF].reshape(D_MODEL, FF_NCHUNK, FF_CHUNK).transpose(1, 0, 2).astype(BF),
        "up_g": w["ffn_w_up"][l][:, D_FF:].reshape(D_MODEL, FF_NCHUNK, FF_CHUNK).transpose(1, 0, 2).astype(BF),
        "conv_w": w["ffn_conv_w"][l].astype(F32).reshape(3, FF_NCHUNK, FF_CHUNK).transpose(1, 0, 2),
        "conv_b": w["ffn_conv_b"][l].astype(F32).reshape(FF_NCHUNK, 1, FF_CHUNK),
        "down": w["ffn_w_down"][l].reshape(FF_NCHUNK, FF_CHUNK, D_MODEL).astype(BF),
    }
    g = w["mix_out_norm_w"][l]
    p["g_a"], p["g_b"], p["g_c"], p["g_d"] = (row(g[i * MIXW:(i + 1) * MIXW]) for i in range(4))
    sm = jax.nn.softmax(w["hgrn_lower_bound"].astype(F32), axis=0)
    lb = jnp.concatenate([jnp.zeros_like(sm[:1]), jnp.cumsum(sm, axis=0)[:-1]], axis=0)[l]
    p["lb_tab"] = jnp.concatenate(
        [row(jnp.log(lb)), row(jnp.log1p(-lb)), row(1.0 - lb), jnp.zeros((SUBLANES - 3, MIXW), F32)], axis=0)
    return p


def _tiles(S):
    return {"tm": 512, "s5_l": 256, "tq": 256, "tk": 512, "hg_tb": 512}


def _encode(x, mem, layers, final_w):
    B, S, _ = x.shape
    t = _tiles(S)
    cos, sin = _rope_tables(S)
    depth = len(layers)
    for l, p in enumerate(layers):
        pp = dict(p, cos=cos, sin=sin)
        qkva, ub, qc, kc, vct, dd = _in_proj(x, pp, t["tm"])
        ma = _na(qkva, p["na_bias"], p["g_a"])
        mb = _s5(ub, p["s5_f"], p["s5_b"], p["s5_d"], p["s5_glu_w"], p["s5_glu_b"], p["g_b"], t["s5_l"])
        mc = _gqa(qc, kc, vct, p["g_c"], t["tq"], t["tk"])
        md = _hgrn(dd, p["lb_tab"], p["g_d"], p["ones"], t["hg_tb"])
        kv = _mem_kv(mem, p["norm_mem_w"], p["xkv"])
        x = _post(x, ma, mb, mc, md, kv, p, t["tm"])
        x = _ffn(x, p, final_w, t["tm"], final=(l == depth - 1))
    return x


def kernel(x_prompt, x_sample, mem_prompt, mem_sample, norm_mix_w, w_in, na_rpb, s5_lambda_re, s5_lambda_im, s5_log_dt, s5_b_re, s5_b_im, s5_c_re, s5_c_im, s5_d, s5_glu_w, s5_glu_b, gqa_q_norm_w, gqa_k_norm_w, hgrn_lower_bound, mix_out_norm_w, w_out, norm_xattn_w, norm_mem_w, xattn_w_q, xattn_w_kv, xattn_w_o, norm_ffn_w, ffn_w_up, ffn_conv_w, ffn_conv_b, ffn_w_down, final_norm_w):
    w = dict(norm_mix_w=norm_mix_w, w_in=w_in, na_rpb=na_rpb, s5_lambda_re=s5_lambda_re,
             s5_lambda_im=s5_lambda_im, s5_log_dt=s5_log_dt, s5_b_re=s5_b_re, s5_b_im=s5_b_im,
             s5_c_re=s5_c_re, s5_c_im=s5_c_im, s5_d=s5_d, s5_glu_w=s5_glu_w, s5_glu_b=s5_glu_b,
             gqa_q_norm_w=gqa_q_norm_w, gqa_k_norm_w=gqa_k_norm_w, hgrn_lower_bound=hgrn_lower_bound,
             mix_out_norm_w=mix_out_norm_w, w_out=w_out, norm_xattn_w=norm_xattn_w, norm_mem_w=norm_mem_w,
             xattn_w_q=xattn_w_q, xattn_w_kv=xattn_w_kv, xattn_w_o=xattn_w_o, norm_ffn_w=norm_ffn_w,
             ffn_w_up=ffn_w_up, ffn_conv_w=ffn_conv_w, ffn_conv_b=ffn_conv_b, ffn_w_down=ffn_w_down)
    layers = [_layer_params(l, w) for l in range(w_in.shape[0])]
    final_w = final_norm_w.reshape(1, -1).astype(F32)
    return (_encode(x_prompt, mem_prompt, layers, final_w), _encode(x_sample, mem_sample, layers, final_w))
```

```python
import functools
import math

import numpy as np
import jax
import jax.numpy as jnp
from jax import lax
from jax.experimental import pallas as pl
from jax.experimental.pallas import tpu as pltpu

D_MODEL = 1024
GRID_W = 64
HEAD_DIM = 64
EPS = 1e-6
NA_HEADS = 4
NA_WIN_ROWS = 8
NA_WIN_COLS = 16
S5_WIDTH = 256
S5_GROUP = 16
S5_GROUPS = 16
S5_STATE = 64
S5_NSTATE = S5_GROUPS * S5_STATE
GQA_Q_HEADS = 4
GQA_KV_HEADS = 2
ROPE_THETA = 10000.0
ROPE_AXIS_DIM = HEAD_DIM // 2
HGRN_HEADS = 4
XA_HEADS = 4
MEM_TOKENS = 256
D_FF = 2816
MIXW = 256
NEG = -1e30
LOG2E = 1.4426950408889634

SUBLANES = 8
FF_CHUNK = 256
FF_NCHUNK = D_FF // FF_CHUNK
HALO = 16
HG_CHUNK = 64
HG_LEVELS = (32, 16, 8)
VMEM_LIMIT = 56 * 1024 * 1024

BF = jnp.bfloat16
F32 = jnp.float32


def _cparams(sem):
    return pltpu.CompilerParams(dimension_semantics=sem, vmem_limit_bytes=VMEM_LIMIT)


def _dot(a, b):
    return jnp.dot(a, b, preferred_element_type=F32)


def _dot_nt(a, b):
    return lax.dot_general(a, b, (((1,), (1,)), ((), ())), preferred_element_type=F32)


def _dot_tn(a, b):
    return lax.dot_general(a, b, (((0,), (0,)), ((), ())), preferred_element_type=F32)


def _split3(x):
    h1 = x.astype(BF)
    r1 = x - h1.astype(F32)
    h2 = r1.astype(BF)
    r2 = r1 - h2.astype(F32)
    return h1, h2, r2.astype(BF)


def _rms(x, w):
    ms = jnp.mean(x * x, axis=-1, keepdims=True)
    return x * lax.rsqrt(ms + EPS) * w


def _head_mask(shape, axis, h):
    idx = lax.broadcasted_iota(jnp.int32, shape, axis)
    return (idx >= h * HEAD_DIM) & (idx < (h + 1) * HEAD_DIM)


def _const_spec(shape):
    nd = len(shape)
    return pl.BlockSpec(shape, lambda *_: (0,) * nd)


def _in_proj_kernel(x_ref, nw_ref, wa_ref, wb_ref, wq_ref, wk_ref, wvt_ref, wd_ref, ones_ref,
                    gq_ref, gk_ref, cos_ref, sin_ref,
                    qkva_ref, ub_ref, qc_ref, kc_ref, vct_ref, dd_ref):
    x = x_ref[...]
    h = _rms(x, nw_ref[...]).astype(BF)
    qkva_ref[...] = _dot(h, wa_ref[...]).astype(BF)
    ub_ref[...] = _dot(h, wb_ref[...])
    dd_ref[...] = _dot(h, wd_ref[...])
    vct_ref[...] = _dot_nt(wvt_ref[...], h).astype(BF)

    ones = ones_ref[...]
    cos = cos_ref[...]
    sin = sin_ref[...]

    def norm_rope(y, g, width):
        yy = y * y
        hi = yy.astype(BF)
        lo = (yy - hi.astype(F32)).astype(BF)
        ms = (_dot(hi, ones[:width, :width]) + _dot(lo, ones[:width, :width])) * (1.0 / HEAD_DIM)
        yn = y * lax.rsqrt(ms + EPS) * g
        lane = lax.broadcasted_iota(jnp.int32, yn.shape, 1)
        first_half = (lane % HEAD_DIM) < ROPE_AXIS_DIM
        swapped = jnp.where(first_half,
                            pltpu.roll(yn, width - ROPE_AXIS_DIM, axis=1),
                            pltpu.roll(yn, ROPE_AXIS_DIM, axis=1))
        return yn * cos[:, :width] + swapped * sin[:, :width]

    q = norm_rope(_dot(h, wq_ref[...]), gq_ref[...], 4 * HEAD_DIM)
    qc_ref[...] = (q * (LOG2E / math.sqrt(HEAD_DIM))).astype(BF)
    k = norm_rope(_dot(h, wk_ref[...]), gk_ref[...][:, :2 * HEAD_DIM], 2 * HEAD_DIM)
    kc_ref[...] = k.astype(BF)


def _in_proj(x, p, tm):
    B, S, _ = x.shape
    grid = (B, S // tm)
    tok = lambda w: pl.BlockSpec((None, tm, w), lambda b, j: (b, j, 0))
    tab = pl.BlockSpec((tm, 4 * HEAD_DIM), lambda b, j: (j, 0))
    ins = [x, p["norm_mix_w"], p["wa"], p["wb"], p["wq"], p["wk"], p["wvt"], p["wd"], p["ones"],
           p["gq"], p["gk"], p["cos"], p["sin"]]
    in_specs = [tok(D_MODEL)] + [_const_spec(a.shape) for a in ins[1:11]] + [tab, tab]
    out_shape = (
        jax.ShapeDtypeStruct((B, S, 768), BF),
        jax.ShapeDtypeStruct((B, S, 256), F32),
        jax.ShapeDtypeStruct((B, S, 256), BF),
        jax.ShapeDtypeStruct((B, S, 128), BF),
        jax.ShapeDtypeStruct((B, 128, S), BF),
        jax.ShapeDtypeStruct((B, S, 1280), F32),
    )
    out_specs = (tok(768), tok(256), tok(256), tok(128),
                 pl.BlockSpec((None, 128, tm), lambda b, j: (b, 0, j)), tok(1280))
    return pl.pallas_call(
        _in_proj_kernel, grid=grid, in_specs=in_specs, out_specs=out_specs, out_shape=out_shape,
        compiler_params=_cparams(("parallel", "parallel")), name="in_proj")(*ins)


NA_CHUNK_ROWS = 8
NA_CHUNK = NA_CHUNK_ROWS * GRID_W
NA_KEYS = NA_WIN_ROWS * GRID_W


def _na_kernel(q_ref, kp_ref, kc_ref, kn_ref, vp_ref, vc_ref, vn_ref, bias_ref, g_ref, o_ref,
               kbuf, vbuf, *, rows):
    j = pl.program_id(1)
    kbuf[0:NA_CHUNK, :] = kp_ref[...]
    kbuf[NA_CHUNK:2 * NA_CHUNK, :] = kc_ref[...]
    kbuf[2 * NA_CHUNK:3 * NA_CHUNK, :] = kn_ref[...]
    for part, ref in enumerate((vp_ref, vc_ref, vn_ref)):
        v = ref[...]
        for h in range(NA_HEADS):
            vbuf[h, part * NA_CHUNK:(part + 1) * NA_CHUNK, :] = jnp.where(
                _head_mask(v.shape, 1, h), v, jnp.zeros_like(v))
    gain = g_ref[...]

    def row_body(rr, carry):
        r = j * NA_CHUNK_ROWS + rr
        r0 = jnp.clip(r - NA_WIN_ROWS // 2, 0, rows - NA_WIN_ROWS)
        off = pl.multiple_of((r0 - j * NA_CHUNK_ROWS + NA_CHUNK_ROWS) * GRID_W, GRID_W)
        variant = r - r0
        qrow = pl.multiple_of(rr * GRID_W, GRID_W)
        q = q_ref[pl.ds(qrow, GRID_W), :]
        kw = kbuf[pl.ds(off, NA_KEYS), :]
        qstack = jnp.concatenate(
            [jnp.where(_head_mask(q.shape, 1, h), q, jnp.zeros_like(q)) for h in range(NA_HEADS)], axis=0)
        s = _dot_nt(qstack, kw) + bias_ref[variant]
        m = jnp.max(s, axis=-1, keepdims=True)
        e = jnp.exp2(s - m)
        p = (e * (1.0 / jnp.sum(e, axis=-1, keepdims=True))).astype(BF)
        acc = _dot(p[0:GRID_W], vbuf[0, pl.ds(off, NA_KEYS), :])
        for h in range(1, NA_HEADS):
            acc = acc + _dot(p[h * GRID_W:(h + 1) * GRID_W], vbuf[h, pl.ds(off, NA_KEYS), :])
        o_ref[pl.ds(qrow, GRID_W), :] = _rms(acc, gain).astype(BF)
        return carry

    lax.fori_loop(0, NA_CHUNK_ROWS, row_body, 0, unroll=2)


def _na_bias_table(rpb):
    qcols = np.arange(GRID_W)
    kcols = np.arange(GRID_W)
    c0 = np.clip(qcols - NA_WIN_COLS // 2, 0, GRID_W - NA_WIN_COLS)
    in_win = (kcols[None, :] >= c0[:, None]) & (kcols[None, :] < c0[:, None] + NA_WIN_COLS)
    dc = np.clip(kcols[None, :] - qcols[:, None], -(NA_WIN_COLS - 1), NA_WIN_COLS - 1) + (NA_WIN_COLS - 1)
    onehot = jnp.asarray(dc[None, :, :] == np.arange(2 * NA_WIN_COLS - 1)[:, None, None], F32)
    cols = jnp.einsum('hrc,cqk->hrqk', rpb, onehot, precision=lax.Precision.HIGHEST) * LOG2E
    cols = jnp.where(in_win[None, None], cols, NEG)
    variants = []
    for var in range(NA_WIN_ROWS):
        lo = NA_WIN_ROWS - 1 - var
        bias = cols[:, lo:lo + NA_WIN_ROWS]
        variants.append(bias.transpose(0, 2, 1, 3).reshape(NA_HEADS * GRID_W, NA_KEYS))
    return jnp.stack(variants)


def _na(qkva, bias, gain):
    B, S, _ = qkva.shape
    rows = S // GRID_W
    nchunk = S // NA_CHUNK
    blk = lambda col, f: pl.BlockSpec((None, NA_CHUNK, MIXW), lambda b, j: (b, f(j), col))
    prev = lambda j: jnp.maximum(j - 1, 0)
    cur = lambda j: j
    nxt = lambda j: jnp.minimum(j + 1, nchunk - 1)
    in_specs = [blk(0, cur), blk(1, prev), blk(1, cur), blk(1, nxt), blk(2, prev), blk(2, cur), blk(2, nxt),
                _const_spec(bias.shape), _const_spec(gain.shape)]
    return pl.pallas_call(
        functools.partial(_na_kernel, rows=rows), grid=(B, nchunk), in_specs=in_specs,
        out_specs=pl.BlockSpec((None, NA_CHUNK, MIXW), lambda b, j: (b, j, 0)),
        out_shape=jax.ShapeDtypeStruct((B, S, MIXW), BF),
        scratch_shapes=[pltpu.VMEM((3 * NA_CHUNK, MIXW), BF), pltpu.VMEM((NA_HEADS, 3 * NA_CHUNK, MIXW), BF)],
        compiler_params=_cparams(("parallel", "parallel")), name="na")(
            qkva, qkva, qkva, qkva, qkva, qkva, qkva, bias, gain)


S5_LANES = 256


def _s5_kernel(*refs, rev, ntile):
    if rev:
        (u_ref, yf_ref, bbd_ref, cc_ref, tab_ref, d_ref, gw_ref, gb_ref, g_ref, o_ref, z_ref, carry_ref) = refs
    else:
        (u_ref, bbd_ref, cc_ref, tab_ref, o_ref, z_ref, carry_ref) = refs
    j = pl.program_id(1)

    @pl.when(j == 0)
    def _():
        carry_ref[...] = jnp.zeros_like(carry_ref)

    u = u_ref[...]
    z_ref[...] = _dot(u.astype(BF), bbd_ref[...])

    for c0 in range(0, S5_NSTATE, S5_LANES):
        re = slice(c0, c0 + S5_LANES)
        im = slice(S5_NSTATE + c0, S5_NSTATE + c0 + S5_LANES)
        steps = [(tab_ref[2 * i, :, re], tab_ref[2 * i + 1, :, re], 1 << i) for i in range(3)]
        pr, pi = tab_ref[6, :, re], tab_ref[7, :, re]

        def tile_body(i, carry):
            cr, ci = carry
            t = (ntile - 1 - i) if rev else i
            row = pl.multiple_of(t * SUBLANES, SUBLANES)
            zr = z_ref[pl.ds(row, SUBLANES), re]
            zi = z_ref[pl.ds(row, SUBLANES), im]
            for ar, ai, sh in steps:
                amt = SUBLANES - sh if rev else sh
                sr = pltpu.roll(zr, amt, axis=0)
                si = pltpu.roll(zi, amt, axis=0)
                zr, zi = zr + ar * sr - ai * si, zi + ar * si + ai * sr
            zr, zi = zr + pr * cr - pi * ci, zi + pr * ci + pi * cr
            z_ref[pl.ds(row, SUBLANES), re] = zr
            z_ref[pl.ds(row, SUBLANES), im] = zi
            last = 0 if rev else SUBLANES - 1
            return (jnp.broadcast_to(zr[last:last + 1, :], zr.shape),
                    jnp.broadcast_to(zi[last:last + 1, :], zi.shape))

        cr, ci = lax.fori_loop(0, ntile, tile_body, (carry_ref[:, re], carry_ref[:, im]), unroll=2)
        carry_ref[:, re] = cr
        carry_ref[:, im] = ci

    y = _dot(z_ref[...].astype(BF), cc_ref[...])
    if not rev:
        o_ref[...] = y
    else:
        y = y + yf_ref[...] + d_ref[...] * u
        hcur = 0.5 * y * (1.0 + jnp.tanh(math.sqrt(2.0 / math.pi) * (y + 0.044715 * (y * y * y))))
        gate = _dot(hcur.astype(BF), gw_ref[...]) + gb_ref[...]
        out = hcur * (1.0 / (1.0 + jnp.exp(-gate)))
        o_ref[...] = _rms(out, g_ref[...]).astype(BF)


def _s5_tables(lam_re, lam_im, log_dt, b_re, b_im, c_re, c_im, rev):
    lre = jnp.minimum(lam_re.astype(F32), -1e-4)
    lim = lam_im.astype(F32)
    dt = jnp.exp(log_dt.astype(F32))[:, None]
    mag = jnp.exp(lre * dt)
    ar = mag * jnp.cos(lim * dt)
    ai = mag * jnp.sin(lim * dt)
    den = lre * lre + lim * lim
    nre = ar - 1.0
    coef_re = (nre * lre + ai * lim) / den
    coef_im = (ai * lre - nre * lim) / den
    bre, bim = b_re.astype(F32), b_im.astype(F32)
    bbar_re = coef_re[..., None] * bre - coef_im[..., None] * bim
    bbar_im = coef_re[..., None] * bim + coef_im[..., None] * bre
    eye = jnp.eye(S5_GROUPS, dtype=F32)
    bd = lambda m: jnp.einsum('gnc,gh->gchn', m, eye).reshape(S5_WIDTH, S5_NSTATE)
    bbd = jnp.concatenate([bd(bbar_re), bd(bbar_im)], axis=1).astype(BF)
    cd = lambda m: jnp.einsum('gcn,gh->gnhc', m.astype(F32), eye).reshape(S5_NSTATE, S5_WIDTH)
    cc = jnp.concatenate([cd(c_re), -cd(c_im)], axis=0).astype(BF)
    a1r, a1i = ar.reshape(1, -1), ai.reshape(1, -1)
    cmul = lambda xr, xi, yr, yi: (xr * yr - xi * yi, xr * yi + xi * yr)
    a2r, a2i = cmul(a1r, a1i, a1r, a1i)
    a4r, a4i = cmul(a2r, a2i, a2r, a2i)
    pw = [(a1r, a1i)]
    for _ in range(SUBLANES - 1):
        pw.append(cmul(pw[-1][0], pw[-1][1], a1r, a1i))
    pr = jnp.concatenate([x[0] for x in pw], axis=0)
    pi = jnp.concatenate([x[1] for x in pw], axis=0)
    rows = np.arange(SUBLANES)[:, None]
    tabs = []
    for (xr, xi), sh in (((a1r, a1i), 1), ((a2r, a2i), 2), ((a4r, a4i), 4)):
        keep = jnp.asarray(rows >= sh, F32)
        tabs += [keep * xr, keep * xi]
    tabs += [pr, pi]
    tab = jnp.stack(tabs)
    if rev:
        tab = tab[:, ::-1, :]
    return bbd, cc, tab


def _s5(u, tabs_f, tabs_b, d_skip, glu_w, glu_b, gain, L):
    B, S, _ = u.shape
    nch = S // L
    ntile = L // SUBLANES
    scratch = [pltpu.VMEM((L, 2 * S5_NSTATE), F32), pltpu.VMEM((SUBLANES, 2 * S5_NSTATE), F32)]
    fwd_blk = pl.BlockSpec((None, L, S5_WIDTH), lambda b, j: (b, j, 0))
    bwd_blk = pl.BlockSpec((None, L, S5_WIDTH), lambda b, j: (b, nch - 1 - j, 0))
    yf = pl.pallas_call(
        functools.partial(_s5_kernel, rev=False, ntile=ntile), grid=(B, nch),
        in_specs=[fwd_blk] + [_const_spec(a.shape) for a in tabs_f],
        out_specs=fwd_blk, out_shape=jax.ShapeDtypeStruct((B, S, S5_WIDTH), F32),
        scratch_shapes=scratch, compiler_params=_cparams(("parallel", "arbitrary")), name="s5_fwd")(u, *tabs_f)
    rest = (d_skip, glu_w, glu_b, gain)
    return pl.pallas_call(
        functools.partial(_s5_kernel, rev=True, ntile=ntile), grid=(B, nch),
        in_specs=[bwd_blk, bwd_blk] + [_const_spec(a.shape) for a in tabs_b + rest],
        out_specs=bwd_blk, out_shape=jax.ShapeDtypeStruct((B, S, S5_WIDTH), BF),
        scratch_shapes=scratch, compiler_params=_cparams(("parallel", "arbitrary")), name="s5_bwd")(
            u, yf, *tabs_b, *rest)


def _gqa_kernel(q_ref, k_ref, vt_ref, g_ref, o_ref, s_ref, p_ref, m_ref, a_ref, a2_ref, l_ref, acc_ref,
                *, tk, nkv):
    q = q_ref[...]
    tq = q.shape[0]
    halves = (q[:, :128], q[:, 128:])
    left = lax.broadcasted_iota(jnp.int32, (tq, 128), 1) < HEAD_DIM
    zero = jnp.zeros((tq, 128), BF)
    qs = [jnp.where(left, halves[0], zero), jnp.where(left, zero, halves[0]),
          jnp.where(left, halves[1], zero), jnp.where(left, zero, halves[1])]
    m_ref[...] = jnp.full(m_ref.shape, -jnp.inf, F32)
    l_ref[...] = jnp.zeros(l_ref.shape, F32)
    acc_ref[...] = jnp.zeros(acc_ref.shape, F32)

    def scores(jb, slot):
        k = k_ref[pl.ds(pl.multiple_of(jb * tk, tk), tk), :]
        for i in range(4):
            st = _dot_nt(k, qs[i])
            s_ref[slot, i] = st
            m_old = m_ref[i:i + 1, :]
            m_new = jnp.maximum(m_old, jnp.max(st, axis=0, keepdims=True))
            a_ref[i:i + 1, :] = jnp.exp2(m_old - m_new)
            m_ref[i:i + 1, :] = m_new

    def probs(slot):
        for i in range(4):
            pt = jnp.exp2(s_ref[slot, i] - m_ref[i:i + 1, :])
            alpha = a_ref[i:i + 1, :]
            l_ref[i:i + 1, :] = alpha * l_ref[i:i + 1, :] + jnp.sum(pt, axis=0, keepdims=True)
            p_ref[slot, i] = pt.astype(BF)
            a2_ref[i:i + 1, :] = alpha

    def values(jb, slot):
        vt = vt_ref[:, pl.ds(pl.multiple_of(jb * tk, tk), tk)]
        for i in range(4):
            g = i % 2
            pv = _dot(vt[g * HEAD_DIM:(g + 1) * HEAD_DIM, :], p_ref[slot, i])
            acc_ref[i] = a2_ref[i:i + 1, :] * acc_ref[i] + pv

    scores(0, 0)
    probs(0)
    scores(1, 1)

    def kv_body(jj, carry):
        jb = 2 * jj
        values(jb, 0)
        probs(1)
        scores(jb + 2, 0)
        values(jb + 1, 1)
        probs(0)
        scores(jb + 3, 1)
        return carry

    lax.fori_loop(0, (nkv - 2) // 2, kv_body, 0)
    values(nkv - 2, (nkv - 2) % 2)
    probs((nkv - 1) % 2)
    values(nkv - 1, (nkv - 1) % 2)
    outs = [acc_ref[i] * (1.0 / l_ref[i:i + 1, :]) for i in range(4)]
    ot = jnp.concatenate([outs[0], outs[2], outs[1], outs[3]], axis=0)
    o_ref[...] = _rms(ot.T, g_ref[...]).astype(BF)


def _gqa(qc, kc, vct, gain, tq, tk):
    B, S, _ = qc.shape
    assert S % (2 * tk) == 0 and S // tk >= 2, "the kv loop handles two blocks per trip"
    return pl.pallas_call(
        functools.partial(_gqa_kernel, tk=tk, nkv=S // tk), grid=(B, S // tq),
        in_specs=[pl.BlockSpec((None, tq, 256), lambda b, j: (b, j, 0)),
                  pl.BlockSpec((None, S, 128), lambda b, j: (b, 0, 0)),
                  pl.BlockSpec((None, 128, S), lambda b, j: (b, 0, 0)),
                  _const_spec(gain.shape)],
        out_specs=pl.BlockSpec((None, tq, MIXW), lambda b, j: (b, j, 0)),
        out_shape=jax.ShapeDtypeStruct((B, S, MIXW), BF),
        scratch_shapes=[pltpu.VMEM((2, 4, tk, tq), F32), pltpu.VMEM((2, 4, tk, tq), BF),
                        pltpu.VMEM((SUBLANES, tq), F32), pltpu.VMEM((SUBLANES, tq), F32),
                        pltpu.VMEM((SUBLANES, tq), F32), pltpu.VMEM((SUBLANES, tq), F32),
                        pltpu.VMEM((4, HEAD_DIM, tq), F32)],
        compiler_params=_cparams(("parallel", "parallel")), name="gqa")(qc, kc, vct, gain)


def _hgrn_level_masks(rev):
    t = np.arange(HG_CHUNK)[:, None]
    s = np.arange(HG_CHUNK)[None, :]
    out = []
    for m in HG_LEVELS:
        bt, bs = t // m, s // m
        ok = ((bt % 2 == 0) & (bs == bt + 1)) if rev else ((bt % 2 == 1) & (bs == bt - 1))
        out.append(np.tile(ok.astype(np.float32), (1, HGRN_HEADS)))
    return np.stack(out)


def _hgrn_kernel(*refs, rev, nchunk):
    if rev:
        (q_ref, z_ref, v_ref, gate_ref, of_ref, lb_ref, tri_ref, ones_ref, lmask_ref, g_ref,
         o_ref, st_ref) = refs
    else:
        (q_ref, z_ref, v_ref, lb_ref, tri_ref, ones_ref, lmask_ref, o_ref, st_ref) = refs
    C = HG_CHUNK
    j = pl.program_id(1)

    @pl.when(j == 0)
    def _():
        st_ref[...] = jnp.zeros_like(st_ref)

    log_lb = lb_ref[0:1, :]
    log1m_lb = lb_ref[1:2, :]
    one_m_lb = lb_ref[2:3, :]
    tri = tri_ref[...]
    ones = ones_ref[...]
    row_in_tile = lax.broadcasted_iota(jnp.int32, (C, MIXW), 0) % SUBLANES
    bd_mask = (lax.broadcasted_iota(jnp.int32, (MIXW, MIXW), 0) // HEAD_DIM
               == lax.broadcasted_iota(jnp.int32, (MIXW, MIXW), 1) // HEAD_DIM)
    zero_row = jnp.zeros((1, MIXW), F32)

    def stack_heads(x):
        xb = x.astype(BF)
        return jnp.concatenate(
            [jnp.where(_head_mask(xb.shape, 1, h), xb, jnp.zeros_like(xb)) for h in range(HGRN_HEADS)], axis=0)

    def chunk_body(ci, carry):
        c = (nchunk - 1 - ci) if rev else ci
        rows = pl.ds(pl.multiple_of(c * C, C), C)
        q = q_ref[rows, :]
        z = z_ref[rows, :]
        v = v_ref[rows, :]
        ls = jnp.minimum(z, 0.0) - jnp.log1p(jnp.exp(-jnp.abs(z)))
        bv = log1m_lb + ls
        logf = jnp.maximum(log_lb, bv) + jnp.log1p(jnp.exp(-jnp.abs(log_lb - bv)))
        kin = one_m_lb * jnp.exp(ls - z)
        p1, p2, p3 = _split3(logf)
        b = _dot(tri, p1) + _dot(tri, p2) + _dot(tri, p3)
        edge = b[0:1, :] if rev else b[C - 1:C, :]

        st = st_ref[...]
        o = _dot_nt((q * jnp.exp(b)).astype(BF), st.astype(BF))

        att = jnp.zeros((C, HGRN_HEADS * C), F32)
        for li, m in enumerate(HG_LEVELS):
            nb = C // m
            qrefs, krefs = [], []
            for blk in range(nb):
                if rev:
                    qrow = (blk + 1) * m
                    krow = blk * m
                else:
                    qrow = blk * m - 1
                    krow = (blk + 1) * m - 1
                qr = zero_row if (qrow < 0 or qrow >= C) else b[qrow:qrow + 1, :]
                qrefs.append(jnp.broadcast_to(qr, (m, MIXW)))
                krefs.append(jnp.broadcast_to(b[krow:krow + 1, :], (m, MIXW)))
            bq = jnp.concatenate(qrefs, axis=0)
            bk = jnp.concatenate(krefs, axis=0)
            ql = (q * jnp.exp(b - bq)).astype(BF)
            kl = kin * jnp.exp(bk - b)
            att = att + lmask_ref[li] * _dot_nt(ql, stack_heads(kl))
        o = o + _dot(att.astype(BF), stack_heads(v))

        for dlt in range(SUBLANES):
            if dlt == 0:
                d = q * kin
                vs = v
            else:
                amt = SUBLANES * (C // SUBLANES) - dlt if rev else dlt
                valid = (row_in_tile + dlt <= SUBLANES - 1) if rev else (row_in_tile >= dlt)
                ks = pltpu.roll(kin, amt, axis=0)
                bs = pltpu.roll(b, amt, axis=0)
                vs = pltpu.roll(v, amt, axis=0)
                decay = jnp.exp(jnp.where(valid, b - bs, 0.0))
                d = jnp.where(valid, q * ks * decay, 0.0)
            o = o + _dot(d.astype(BF), ones) * vs

        kb = (kin * jnp.exp(edge - b)).astype(BF)
        upd = _dot_tn(v.astype(BF), kb)
        st_ref[...] = st * jnp.exp(edge) + jnp.where(bd_mask, upd, 0.0)

        if rev:
            tot = o + of_ref[rows, :]
            gate = gate_ref[rows, :]
            silu = gate * (1.0 / (1.0 + jnp.exp(-gate)))
            o_ref[rows, :] = (_rms(tot, g_ref[...]) * silu).astype(BF)
        else:
            o_ref[rows, :] = o
        return carry

    lax.fori_loop(0, nchunk, chunk_body, 0)


def _hgrn(dd, lb_tab, gain, ones, tb):
    B, S, _ = dd.shape
    nblk = S // tb
    nchunk = tb // HG_CHUNK
    tri_f = jnp.asarray(np.tril(np.ones((HG_CHUNK, HG_CHUNK), np.float32)), BF)
    tri_b = jnp.asarray(np.triu(np.ones((HG_CHUNK, HG_CHUNK), np.float32)), BF)
    lm_f = jnp.asarray(_hgrn_level_masks(False))
    lm_b = jnp.asarray(_hgrn_level_masks(True))
    fcol = lambda col: pl.BlockSpec((None, tb, MIXW), lambda b, j: (b, j, col))
    bcol = lambda col: pl.BlockSpec((None, tb, MIXW), lambda b, j: (b, nblk - 1 - j, col))
    scratch = [pltpu.VMEM((MIXW, MIXW), F32)]
    consts_f = (lb_tab, tri_f, ones, lm_f)
    of = pl.pallas_call(
        functools.partial(_hgrn_kernel, rev=False, nchunk=nchunk), grid=(B, nblk),
        in_specs=[fcol(0), fcol(1), fcol(3)] + [_const_spec(a.shape) for a in consts_f],
        out_specs=fcol(0), out_shape=jax.ShapeDtypeStruct((B, S, MIXW), F32),
        scratch_shapes=scratch, compiler_params=_cparams(("parallel", "arbitrary")), name="hgrn_fwd")(
            dd, dd, dd, *consts_f)
    consts_b = (lb_tab, tri_b, ones, lm_b, gain)
    return pl.pallas_call(
        functools.partial(_hgrn_kernel, rev=True, nchunk=nchunk), grid=(B, nblk),
        in_specs=[bcol(0), bcol(2), bcol(3), bcol(4), bcol(0)] + [_const_spec(a.shape) for a in consts_b],
        out_specs=bcol(0), out_shape=jax.ShapeDtypeStruct((B, S, MIXW), BF),
        scratch_shapes=scratch, compiler_params=_cparams(("parallel", "arbitrary")), name="hgrn_bwd")(
            dd, dd, dd, dd, of, *consts_b)


def _mem_kv_kernel(mem_ref, nw_ref, wkv_ref, o_ref):
    o_ref[...] = _dot(_rms(mem_ref[...], nw_ref[...]).astype(BF), wkv_ref[...]).astype(BF)


def _mem_kv(mem, nw, wkv):
    B, M, _ = mem.shape
    return pl.pallas_call(
        _mem_kv_kernel, grid=(B,),
        in_specs=[pl.BlockSpec((None, M, D_MODEL), lambda b: (b, 0, 0)), _const_spec(nw.shape),
                  _const_spec(wkv.shape)],
        out_specs=pl.BlockSpec((None, M, 2 * MIXW), lambda b: (b, 0, 0)),
        out_shape=jax.ShapeDtypeStruct((B, M, 2 * MIXW), BF),
        compiler_params=_cparams(("parallel",)), name="mem_kv")(mem, nw, wkv)


def _post_kernel(x_ref, ma_ref, mb_ref, mc_ref, md_ref, wo_ref, nx_ref, wq_ref, kv_ref, wxo_ref, o_ref):
    x1 = x_ref[...]
    for i, m_ref in enumerate((ma_ref, mb_ref, mc_ref, md_ref)):
        x1 = x1 + _dot(m_ref[...], wo_ref[i * MIXW:(i + 1) * MIXW, :])
    hq = _rms(x1, nx_ref[...]).astype(BF)
    q = _dot(hq, wq_ref[...]).astype(BF)
    kmem = kv_ref[:, :MIXW]
    vmem = kv_ref[:, MIXW:]
    acc = jnp.zeros(q.shape, F32)
    for h in range(XA_HEADS):
        qm = jnp.where(_head_mask(q.shape, 1, h), q, jnp.zeros_like(q))
        s = _dot_nt(qm, kmem)
        m = jnp.max(s, axis=-1, keepdims=True)
        e = jnp.exp(s - m)
        p = (e * (1.0 / jnp.sum(e, axis=-1, keepdims=True))).astype(BF)
        vm = jnp.where(_head_mask(vmem.shape, 1, h), vmem, jnp.zeros_like(vmem))
        acc = acc + _dot(p, vm)
    o_ref[...] = x1 + _dot(acc.astype(BF), wxo_ref[...])


def _post(x, ma, mb, mc, md, kv, p, tm):
    B, S, _ = x.shape
    tok = lambda w: pl.BlockSpec((None, tm, w), lambda b, j: (b, j, 0))
    consts = (p["w_out"], p["norm_xattn_w"], p["xq"])
    return pl.pallas_call(
        _post_kernel, grid=(B, S // tm),
        in_specs=[tok(D_MODEL), tok(MIXW), tok(MIXW), tok(MIXW), tok(MIXW)]
        + [_const_spec(a.shape) for a in consts]
        + [pl.BlockSpec((None, MEM_TOKENS, 2 * MIXW), lambda b, j: (b, 0, 0)), _const_spec(p["xo"].shape)],
        out_specs=tok(D_MODEL), out_shape=jax.ShapeDtypeStruct(x.shape, F32),
        compiler_params=_cparams(("parallel", "parallel")), name="post")(
            x, ma, mb, mc, md, *consts, kv, p["xo"])


def _ffn_kernel(xp_ref, xc_ref, xn_ref, nw_ref, wa_ref, wg_ref, cw_ref, cb_ref, wd_ref, fw_ref, o_ref,
                hn_ref, gx_ref, a_ref, acc_ref, act_ref, *, tm, final):
    j = pl.program_id(1)
    nj = pl.num_programs(1)
    nw = nw_ref[...]
    hp = _rms(xp_ref[...], nw)
    hnx = _rms(xn_ref[...], nw)
    hn_ref[0:HALO, :] = jnp.where(j > 0, hp, 0.0).astype(BF)
    hn_ref[HALO:HALO + tm, :] = _rms(xc_ref[...], nw).astype(BF)
    hn_ref[HALO + tm:2 * HALO + tm, :] = jnp.where(j < nj - 1, hnx, 0.0).astype(BF)
    acc_ref[...] = jnp.zeros_like(acc_ref)

    def up(c, slot):
        a_ref[slot] = _dot(hn_ref[HALO:HALO + tm, :], wa_ref[c])
        gx_ref[slot] = _dot(hn_ref[...], wg_ref[c])

    def activate(c, slot):
        cw = cw_ref[c]
        gc = (cb_ref[c] + gx_ref[slot, HALO - 1:HALO - 1 + tm, :] * cw[0:1, :]
              + gx_ref[slot, HALO:HALO + tm, :] * cw[1:2, :]
              + gx_ref[slot, HALO + 1:HALO + 1 + tm, :] * cw[2:3, :])
        act_ref[slot] = (gc * (1.0 / (1.0 + jnp.exp(-gc))) * a_ref[slot]).astype(BF)

    def down(c, slot):
        acc_ref[...] += _dot(act_ref[slot], wd_ref[c])

    up(0, 0)
    up(1, 1)
    activate(0, 0)

    for c in range(FF_NCHUNK - 2):
        down(c, c % 2)
        activate(c + 1, (c + 1) % 2)
        up(c + 2, c % 2)
    down(FF_NCHUNK - 2, (FF_NCHUNK - 2) % 2)
    activate(FF_NCHUNK - 1, (FF_NCHUNK - 1) % 2)
    down(FF_NCHUNK - 1, (FF_NCHUNK - 1) % 2)
    y = xc_ref[...] + acc_ref[...]
    if final:
        y = _rms(y, fw_ref[...])
    o_ref[...] = y


def _ffn(x, p, final_w, tm, final):
    B, S, _ = x.shape
    per = tm // HALO
    nh = S // HALO
    consts = (p["norm_ffn_w"], p["up_a"], p["up_g"], p["conv_w"], p["conv_b"], p["down"], final_w)
    return pl.pallas_call(
        functools.partial(_ffn_kernel, tm=tm, final=final), grid=(B, S // tm),
        in_specs=[pl.BlockSpec((None, HALO, D_MODEL), lambda b, j: (b, jnp.maximum(j * per - 1, 0), 0)),
                  pl.BlockSpec((None, tm, D_MODEL), lambda b, j: (b, j, 0)),
                  pl.BlockSpec((None, HALO, D_MODEL), lambda b, j: (b, jnp.minimum((j + 1) * per, nh - 1), 0))]
        + [_const_spec(a.shape) for a in consts],
        out_specs=pl.BlockSpec((None, tm, D_MODEL), lambda b, j: (b, j, 0)),
        out_shape=jax.ShapeDtypeStruct(x.shape, F32),
        scratch_shapes=[pltpu.VMEM((tm + 2 * HALO, D_MODEL), BF), pltpu.VMEM((2, tm + 2 * HALO, FF_CHUNK), F32),
                        pltpu.VMEM((2, tm, FF_CHUNK), F32), pltpu.VMEM((tm, D_MODEL), F32),
                        pltpu.VMEM((2, tm, FF_CHUNK), BF)],
        compiler_params=_cparams(("parallel", "parallel")), name="ffn")(x, x, x, *consts)


def _rope_tables(S):
    t = jnp.arange(S)
    inv = 1.0 / (ROPE_THETA ** (jnp.arange(0, ROPE_AXIS_DIM, 2, dtype=F32) / ROPE_AXIS_DIM))
    ang = jnp.concatenate([(t // GRID_W).astype(F32)[:, None] * inv,
                           (t % GRID_W).astype(F32)[:, None] * inv], axis=-1)
    cos, sin = jnp.cos(ang), jnp.sin(ang)
    cos_h = jnp.concatenate([cos, cos], axis=-1)
    sin_h = jnp.concatenate([-sin, sin], axis=-1)
    return jnp.tile(cos_h, (1, GQA_Q_HEADS)), jnp.tile(sin_h, (1, GQA_Q_HEADS))


def _layer_params(l, w):
    row = lambda a: a.reshape(1, -1).astype(F32)
    w_in = w["w_in"][l]
    qscale = 1.0 / math.sqrt(HEAD_DIM)
    wa = jnp.concatenate([w_in[:, :256] * (qscale * LOG2E), w_in[:, 256:768]], axis=1)
    wq = w_in[:, 1024:1280].reshape(D_MODEL, GQA_Q_HEADS, HEAD_DIM)[:, np.array([0, 2, 1, 3]), :].reshape(
        D_MODEL, 256)
    blk = np.arange(256) // HEAD_DIM
    p = {
        "norm_mix_w": row(w["norm_mix_w"][l]),
        "wa": wa.astype(BF),
        "wb": w_in[:, 768:1024].astype(BF),
        "wq": wq.astype(BF),
        "wk": w_in[:, 1280:1408].astype(BF),
        "wvt": w_in[:, 1408:1536].T.astype(BF),
        "wd": w_in[:, 1536:].astype(BF),
        "ones": jnp.asarray(blk[:, None] == blk[None, :], BF),
        "gq": row(jnp.tile(w["gqa_q_norm_w"][l], GQA_Q_HEADS)),
        "gk": row(jnp.tile(w["gqa_k_norm_w"][l], GQA_Q_HEADS)),
        "na_bias": _na_bias_table(w["na_rpb"][l].astype(F32)),
        "s5_f": _s5_tables(*[w[k][l, 0] for k in ("s5_lambda_re", "s5_lambda_im", "s5_log_dt", "s5_b_re",
                                                    "s5_b_im", "s5_c_re", "s5_c_im")], rev=False),
        "s5_b": _s5_tables(*[w[k][l, 1] for k in ("s5_lambda_re", "s5_lambda_im", "s5_log_dt", "s5_b_re",
                                                    "s5_b_im", "s5_c_re", "s5_c_im")], rev=True),
        "s5_d": row(w["s5_d"][l]),
        "s5_glu_w": w["s5_glu_w"][l].astype(BF),
        "s5_glu_b": row(w["s5_glu_b"][l]),
        "w_out": w["w_out"][l].astype(BF),
        "norm_xattn_w": row(w["norm_xattn_w"][l]),
        "norm_mem_w": row(w["norm_mem_w"][l]),
        "xq": (w["xattn_w_q"][l] * qscale).astype(BF),
        "xkv": w["xattn_w_kv"][l].astype(BF),
        "xo": w["xattn_w_o"][l].astype(BF),
        "norm_ffn_w": row(w["norm_ffn_w"][l]),
        "up_a": w["ffn_w_up"][l][:, :D_FF].reshape(D_MODEL, FF_NCHUNK, FF_CHUNK).transpose(1, 0, 2).astype(BF),
        "up_g": w["ffn_w_up"][l][:, D_FF:].reshape(D_MODEL, FF_NCHUNK, FF_CHUNK).transpose(1, 0, 2).astype(BF),
        "conv_w": w["ffn_conv_w"][l].astype(F32).reshape(3, FF_NCHUNK, FF_CHUNK).transpose(1, 0, 2),
        "conv_b": w["ffn_conv_b"][l].astype(F32).reshape(FF_NCHUNK, 1, FF_CHUNK),
        "down": w["ffn_w_down"][l].reshape(FF_NCHUNK, FF_CHUNK, D_MODEL).astype(BF),
    }
    g = w["mix_out_norm_w"][l]
    p["g_a"], p["g_b"], p["g_c"], p["g_d"] = (row(g[i * MIXW:(i + 1) * MIXW]) for i in range(4))
    sm = jax.nn.softmax(w["hgrn_lower_bound"].astype(F32), axis=0)
    lb = jnp.concatenate([jnp.zeros_like(sm[:1]), jnp.cumsum(sm, axis=0)[:-1]], axis=0)[l]
    p["lb_tab"] = jnp.concatenate(
        [row(jnp.log(lb)), row(jnp.log1p(-lb)), row(1.0 - lb), jnp.zeros((SUBLANES - 3, MIXW), F32)], axis=0)
    return p


def _tiles(S):
    return {"tm": 512, "s5_l": 256, "tq": 256, "tk": 512, "hg_tb": 512}


def _encode(x, mem, layers, final_w):
    B, S, _ = x.shape
    t = _tiles(S)
    cos, sin = _rope_tables(S)
    depth = len(layers)
    for l, p in enumerate(layers):
        pp = dict(p, cos=cos, sin=sin)
        qkva, ub, qc, kc, vct, dd = _in_proj(x, pp, t["tm"])
        ma = _na(qkva, p["na_bias"], p["g_a"])
        mb = _s5(ub, p["s5_f"], p["s5_b"], p["s5_d"], p["s5_glu_w"], p["s5_glu_b"], p["g_b"], t["s5_l"])
        mc = _gqa(qc, kc, vct, p["g_c"], t["tq"], t["tk"])
        md = _hgrn(dd, p["lb_tab"], p["g_d"], p["ones"], t["hg_tb"])
        kv = _mem_kv(mem, p["norm_mem_w"], p["xkv"])
        x = _post(x, ma, mb, mc, md, kv, p, t["tm"])
        x = _ffn(x, p, final_w, t["tm"], final=(l == depth - 1))
    return x


def kernel(x_prompt, x_sample, mem_prompt, mem_sample, norm_mix_w, w_in, na_rpb, s5_lambda_re, s5_lambda_im, s5_log_dt, s5_b_re, s5_b_im, s5_c_re, s5_c_im, s5_d, s5_glu_w, s5_glu_b, gqa_q_norm_w, gqa_k_norm_w, hgrn_lower_bound, mix_out_norm_w, w_out, norm_xattn_w, norm_mem_w, xattn_w_q, xattn_w_kv, xattn_w_o, norm_ffn_w, ffn_w_up, ffn_conv_w, ffn_conv_b, ffn_w_down, final_norm_w):
    w = dict(norm_mix_w=norm_mix_w, w_in=w_in, na_rpb=na_rpb, s5_lambda_re=s5_lambda_re,
             s5_lambda_im=s5_lambda_im, s5_log_dt=s5_log_dt, s5_b_re=s5_b_re, s5_b_im=s5_b_im,
             s5_c_re=s5_c_re, s5_c_im=s5_c_im, s5_d=s5_d, s5_glu_w=s5_glu_w, s5_glu_b=s5_glu_b,
             gqa_q_norm_w=gqa_q_norm_w, gqa_k_norm_w=gqa_k_norm_w, hgrn_lower_bound=hgrn_lower_bound,
             mix_out_norm_w=mix_out_norm_w, w_out=w_out, norm_xattn_w=norm_xattn_w, norm_mem_w=norm_mem_w,
             xattn_w_q=xattn_w_q, xattn_w_kv=xattn_w_kv, xattn_w_o=xattn_w_o, norm_ffn_w=norm_ffn_w,
             ffn_w_up=ffn_w_up, ffn_conv_w=ffn_conv_w, ffn_conv_b=ffn_conv_b, ffn_w_down=ffn_w_down)
    layers = [_layer_params(l, w) for l in range(w_in.shape[0])]
    final_w = final_norm_w.reshape(1, -1).astype(F32)
    return (_encode(x_prompt, mem_prompt, layers, final_w), _encode(x_sample, mem_sample, layers, final_w))
```

```python
import functools
import math

import numpy as np
import jax
import jax.numpy as jnp
from jax import lax
from jax.experimental import pallas as pl
from jax.experimental.pallas import tpu as pltpu

D_MODEL = 1024
GRID_W = 64
HEAD_DIM = 64
EPS = 1e-6
NA_HEADS = 4
NA_WIN_ROWS = 8
NA_WIN_COLS = 16
S5_WIDTH = 256
S5_GROUP = 16
S5_GROUPS = 16
S5_STATE = 64
S5_NSTATE = S5_GROUPS * S5_STATE
GQA_Q_HEADS = 4
GQA_KV_HEADS = 2
ROPE_THETA = 10000.0
ROPE_AXIS_DIM = HEAD_DIM // 2
HGRN_HEADS = 4
XA_HEADS = 4
MEM_TOKENS = 256
D_FF = 2816
MIXW = 256
NEG = -1e30
LOG2E = 1.4426950408889634

SUBLANES = 8
FF_CHUNK = 256
FF_NCHUNK = D_FF // FF_CHUNK
HALO = 16
HG_CHUNK = 64
HG_LEVELS = (32, 16, 8)
VMEM_LIMIT = 56 * 1024 * 1024

BF = jnp.bfloat16
F32 = jnp.float32


def _cparams(sem):
    return pltpu.CompilerParams(dimension_semantics=sem, vmem_limit_bytes=VMEM_LIMIT)


def _dot(a, b):
    return jnp.dot(a, b, preferred_element_type=F32)


def _dot_nt(a, b):
    return lax.dot_general(a, b, (((1,), (1,)), ((), ())), preferred_element_type=F32)


def _dot_tn(a, b):
    return lax.dot_general(a, b, (((0,), (0,)), ((), ())), preferred_element_type=F32)


def _split3(x):
    h1 = x.astype(BF)
    r1 = x - h1.astype(F32)
    h2 = r1.astype(BF)
    r2 = r1 - h2.astype(F32)
    return h1, h2, r2.astype(BF)


def _rms(x, w):
    ms = jnp.mean(x * x, axis=-1, keepdims=True)
    return x * lax.rsqrt(ms + EPS) * w


def _head_mask(shape, axis, h):
    idx = lax.broadcasted_iota(jnp.int32, shape, axis)
    return (idx >= h * HEAD_DIM) & (idx < (h + 1) * HEAD_DIM)


def _const_spec(shape):
    nd = len(shape)
    return pl.BlockSpec(shape, lambda *_: (0,) * nd)


def _in_proj_kernel(x_ref, nw_ref, wa_ref, wb_ref, wq_ref, wk_ref, wvt_ref, wd_ref, ones_ref,
                    gq_ref, gk_ref, cos_ref, sin_ref,
                    qkva_ref, ub_ref, qc_ref, kc_ref, vct_ref, dd_ref):
    x = x_ref[...]
    h = _rms(x, nw_ref[...]).astype(BF)
    qkva_ref[...] = _dot(h, wa_ref[...]).astype(BF)
    ub_ref[...] = _dot(h, wb_ref[...])
    dd_ref[...] = _dot(h, wd_ref[...])
    vct_ref[...] = _dot_nt(wvt_ref[...], h).astype(BF)

    ones = ones_ref[...]
    cos = cos_ref[...]
    sin = sin_ref[...]

    def norm_rope(y, g, width):
        yy = y * y
        hi = yy.astype(BF)
        lo = (yy - hi.astype(F32)).astype(BF)
        ms = (_dot(hi, ones[:width, :width]) + _dot(lo, ones[:width, :width])) * (1.0 / HEAD_DIM)
        yn = y * lax.rsqrt(ms + EPS) * g
        lane = lax.broadcasted_iota(jnp.int32, yn.shape, 1)
        first_half = (lane % HEAD_DIM) < ROPE_AXIS_DIM
        swapped = jnp.where(first_half,
                            pltpu.roll(yn, width - ROPE_AXIS_DIM, axis=1),
                            pltpu.roll(yn, ROPE_AXIS_DIM, axis=1))
        return yn * cos[:, :width] + swapped * sin[:, :width]

    q = norm_rope(_dot(h, wq_ref[...]), gq_ref[...], 4 * HEAD_DIM)
    qc_ref[...] = (q * (LOG2E / math.sqrt(HEAD_DIM))).astype(BF)
    k = norm_rope(_dot(h, wk_ref[...]), gk_ref[...][:, :2 * HEAD_DIM], 2 * HEAD_DIM)
    kc_ref[...] = k.astype(BF)


def _in_proj(x, p, tm):
    B, S, _ = x.shape
    grid = (B, S // tm)
    tok = lambda w: pl.BlockSpec((None, tm, w), lambda b, j: (b, j, 0))
    tab = pl.BlockSpec((tm, 4 * HEAD_DIM), lambda b, j: (j, 0))
    ins = [x, p["norm_mix_w"], p["wa"], p["wb"], p["wq"], p["wk"], p["wvt"], p["wd"], p["ones"],
           p["gq"], p["gk"], p["cos"], p["sin"]]
    in_specs = [tok(D_MODEL)] + [_const_spec(a.shape) for a in ins[1:11]] + [tab, tab]
    out_shape = (
        jax.ShapeDtypeStruct((B, S, 768), BF),
        jax.ShapeDtypeStruct((B, S, 256), F32),
        jax.ShapeDtypeStruct((B, S, 256), BF),
        jax.ShapeDtypeStruct((B, S, 128), BF),
        jax.ShapeDtypeStruct((B, 128, S), BF),
        jax.ShapeDtypeStruct((B, S, 1280), F32),
    )
    out_specs = (tok(768), tok(256), tok(256), tok(128),
                 pl.BlockSpec((None, 128, tm), lambda b, j: (b, 0, j)), tok(1280))
    return pl.pallas_call(
        _in_proj_kernel, grid=grid, in_specs=in_specs, out_specs=out_specs, out_shape=out_shape,
        compiler_params=_cparams(("parallel", "parallel")), name="in_proj")(*ins)


NA_CHUNK_ROWS = 8
NA_CHUNK = NA_CHUNK_ROWS * GRID_W
NA_KEYS = NA_WIN_ROWS * GRID_W


def _na_kernel(q_ref, kp_ref, kc_ref, kn_ref, vp_ref, vc_ref, vn_ref, bias_ref, g_ref, o_ref,
               kbuf, vbuf, *, rows):
    j = pl.program_id(1)
    kbuf[0:NA_CHUNK, :] = kp_ref[...]
    kbuf[NA_CHUNK:2 * NA_CHUNK, :] = kc_ref[...]
    kbuf[2 * NA_CHUNK:3 * NA_CHUNK, :] = kn_ref[...]
    for part, ref in enumerate((vp_ref, vc_ref, vn_ref)):
        v = ref[...]
        for h in range(NA_HEADS):
            vbuf[h, part * NA_CHUNK:(part + 1) * NA_CHUNK, :] = jnp.where(
                _head_mask(v.shape, 1, h), v, jnp.zeros_like(v))
    gain = g_ref[...]

    def row_body(rr, carry):
        r = j * NA_CHUNK_ROWS + rr
        r0 = jnp.clip(r - NA_WIN_ROWS // 2, 0, rows - NA_WIN_ROWS)
        off = pl.multiple_of((r0 - j * NA_CHUNK_ROWS + NA_CHUNK_ROWS) * GRID_W, GRID_W)
        variant = r - r0
        qrow = pl.multiple_of(rr * GRID_W, GRID_W)
        q = q_ref[pl.ds(qrow, GRID_W), :]
        kw = kbuf[pl.ds(off, NA_KEYS), :]
        qstack = jnp.concatenate(
            [jnp.where(_head_mask(q.shape, 1, h), q, jnp.zeros_like(q)) for h in range(NA_HEADS)], axis=0)
        s = _dot_nt(qstack, kw) + bias_ref[variant]
        m = jnp.max(s, axis=-1, keepdims=True)
        e = jnp.exp2(s - m)
        p = (e * (1.0 / jnp.sum(e, axis=-1, keepdims=True))).astype(BF)
        acc = _dot(p[0:GRID_W], vbuf[0, pl.ds(off, NA_KEYS), :])
        for h in range(1, NA_HEADS):
            acc = acc + _dot(p[h * GRID_W:(h + 1) * GRID_W], vbuf[h, pl.ds(off, NA_KEYS), :])
        o_ref[pl.ds(qrow, GRID_W), :] = _rms(acc, gain).astype(BF)
        return carry

    lax.fori_loop(0, NA_CHUNK_ROWS, row_body, 0, unroll=2)


def _na_bias_table(rpb):
    qcols = np.arange(GRID_W)
    kcols = np.arange(GRID_W)
    c0 = np.clip(qcols - NA_WIN_COLS // 2, 0, GRID_W - NA_WIN_COLS)
    in_win = (kcols[None, :] >= c0[:, None]) & (kcols[None, :] < c0[:, None] + NA_WIN_COLS)
    dc = np.clip(kcols[None, :] - qcols[:, None], -(NA_WIN_COLS - 1), NA_WIN_COLS - 1) + (NA_WIN_COLS - 1)
    onehot = jnp.asarray(dc[None, :, :] == np.arange(2 * NA_WIN_COLS - 1)[:, None, None], F32)
    cols = jnp.einsum('hrc,cqk->hrqk', rpb, onehot, precision=lax.Precision.HIGHEST) * LOG2E
    cols = jnp.where(in_win[None, None], cols, NEG)
    variants = []
    for var in range(NA_WIN_ROWS):
        lo = NA_WIN_ROWS - 1 - var
        bias = cols[:, lo:lo + NA_WIN_ROWS]
        variants.append(bias.transpose(0, 2, 1, 3).reshape(NA_HEADS * GRID_W, NA_KEYS))
    return jnp.stack(variants)


def _na(qkva, bias, gain):
    B, S, _ = qkva.shape
    rows = S // GRID_W
    nchunk = S // NA_CHUNK
    blk = lambda col, f: pl.BlockSpec((None, NA_CHUNK, MIXW), lambda b, j: (b, f(j), col))
    prev = lambda j: jnp.maximum(j - 1, 0)
    cur = lambda j: j
    nxt = lambda j: jnp.minimum(j + 1, nchunk - 1)
    in_specs = [blk(0, cur), blk(1, prev), blk(1, cur), blk(1, nxt), blk(2, prev), blk(2, cur), blk(2, nxt),
                _const_spec(bias.shape), _const_spec(gain.shape)]
    return pl.pallas_call(
        functools.partial(_na_kernel, rows=rows), grid=(B, nchunk), in_specs=in_specs,
        out_specs=pl.BlockSpec((None, NA_CHUNK, MIXW), lambda b, j: (b, j, 0)),
        out_shape=jax.ShapeDtypeStruct((B, S, MIXW), BF),
        scratch_shapes=[pltpu.VMEM((3 * NA_CHUNK, MIXW), BF), pltpu.VMEM((NA_HEADS, 3 * NA_CHUNK, MIXW), BF)],
        compiler_params=_cparams(("parallel", "parallel")), name="na")(
            qkva, qkva, qkva, qkva, qkva, qkva, qkva, bias, gain)


S5_LANES = 256


def _s5_kernel(*refs, rev, ntile):
    if rev:
        (u_ref, yf_ref, bbd_ref, cc_ref, tab_ref, d_ref, gw_ref, gb_ref, g_ref, o_ref, z_ref, carry_ref) = refs
    else:
        (u_ref, bbd_ref, cc_ref, tab_ref, o_ref, z_ref, carry_ref) = refs
    j = pl.program_id(1)

    @pl.when(j == 0)
    def _():
        carry_ref[...] = jnp.zeros_like(carry_ref)

    u = u_ref[...]
    ub = u.astype(BF)
    y = None
    for c0 in range(0, S5_NSTATE, S5_LANES):
        re = slice(c0, c0 + S5_LANES)
        im = slice(S5_NSTATE + c0, S5_NSTATE + c0 + S5_LANES)
        z_ref[:, re] = _dot(ub, bbd_ref[:, re])
        z_ref[:, im] = _dot(ub, bbd_ref[:, im])
        steps = [(tab_ref[2 * i, :, re], tab_ref[2 * i + 1, :, re], 1 << i) for i in range(3)]
        pr, pi = tab_ref[6, :, re], tab_ref[7, :, re]
        cr, ci = carry_ref[:, re], carry_ref[:, im]
        for i in range(ntile):
            t = (ntile - 1 - i) if rev else i
            rows = slice(t * SUBLANES, (t + 1) * SUBLANES)
            zr = z_ref[rows, re]
            zi = z_ref[rows, im]
            for ar, ai, sh in steps:
                amt = SUBLANES - sh if rev else sh
                sr = pltpu.roll(zr, amt, axis=0)
                si = pltpu.roll(zi, amt, axis=0)
                zr, zi = zr + ar * sr - ai * si, zi + ar * si + ai * sr
            zr, zi = zr + pr * cr - pi * ci, zi + pr * ci + pi * cr
            z_ref[rows, re] = zr
            z_ref[rows, im] = zi
            last = 0 if rev else SUBLANES - 1
            cr = jnp.broadcast_to(zr[last:last + 1, :], zr.shape)
            ci = jnp.broadcast_to(zi[last:last + 1, :], zi.shape)
        carry_ref[:, re] = cr
        carry_ref[:, im] = ci
        part = _dot(z_ref[:, re].astype(BF), cc_ref[re, :]) + _dot(z_ref[:, im].astype(BF), cc_ref[im, :])
        y = part if y is None else y + part
    if not rev:
        o_ref[...] = y
    else:
        y = y + yf_ref[...] + d_ref[...] * u
        hcur = 0.5 * y * (1.0 + jnp.tanh(math.sqrt(2.0 / math.pi) * (y + 0.044715 * (y * y * y))))
        gate = _dot(hcur.astype(BF), gw_ref[...]) + gb_ref[...]
        out = hcur * (1.0 / (1.0 + jnp.exp(-gate)))
        o_ref[...] = _rms(out, g_ref[...]).astype(BF)


def _s5_tables(lam_re, lam_im, log_dt, b_re, b_im, c_re, c_im, rev):
    lre = jnp.minimum(lam_re.astype(F32), -1e-4)
    lim = lam_im.astype(F32)
    dt = jnp.exp(log_dt.astype(F32))[:, None]
    mag = jnp.exp(lre * dt)
    ar = mag * jnp.cos(lim * dt)
    ai = mag * jnp.sin(lim * dt)
    den = lre * lre + lim * lim
    nre = ar - 1.0
    coef_re = (nre * lre + ai * lim) / den
    coef_im = (ai * lre - nre * lim) / den
    bre, bim = b_re.astype(F32), b_im.astype(F32)
    bbar_re = coef_re[..., None] * bre - coef_im[..., None] * bim
    bbar_im = coef_re[..., None] * bim + coef_im[..., None] * bre
    eye = jnp.eye(S5_GROUPS, dtype=F32)
    bd = lambda m: jnp.einsum('gnc,gh->gchn', m, eye).reshape(S5_WIDTH, S5_NSTATE)
    bbd = jnp.concatenate([bd(bbar_re), bd(bbar_im)], axis=1).astype(BF)
    cd = lambda m: jnp.einsum('gcn,gh->gnhc', m.astype(F32), eye).reshape(S5_NSTATE, S5_WIDTH)
    cc = jnp.concatenate([cd(c_re), -cd(c_im)], axis=0).astype(BF)
    a1r, a1i = ar.reshape(1, -1), ai.reshape(1, -1)
    cmul = lambda xr, xi, yr, yi: (xr * yr - xi * yi, xr * yi + xi * yr)
    a2r, a2i = cmul(a1r, a1i, a1r, a1i)
    a4r, a4i = cmul(a2r, a2i, a2r, a2i)
    pw = [(a1r, a1i)]
    for _ in range(SUBLANES - 1):
        pw.append(cmul(pw[-1][0], pw[-1][1], a1r, a1i))
    pr = jnp.concatenate([x[0] for x in pw], axis=0)
    pi = jnp.concatenate([x[1] for x in pw], axis=0)
    rows = np.arange(SUBLANES)[:, None]
    tabs = []
    for (xr, xi), sh in (((a1r, a1i), 1), ((a2r, a2i), 2), ((a4r, a4i), 4)):
        keep = jnp.asarray(rows >= sh, F32)
        tabs += [keep * xr, keep * xi]
    tabs += [pr, pi]
    tab = jnp.stack(tabs)
    if rev:
        tab = tab[:, ::-1, :]
    return bbd, cc, tab


def _s5(u, tabs_f, tabs_b, d_skip, glu_w, glu_b, gain, L):
    B, S, _ = u.shape
    nch = S // L
    ntile = L // SUBLANES
    scratch = [pltpu.VMEM((L, 2 * S5_NSTATE), F32), pltpu.VMEM((SUBLANES, 2 * S5_NSTATE), F32)]
    fwd_blk = pl.BlockSpec((None, L, S5_WIDTH), lambda b, j: (b, j, 0))
    bwd_blk = pl.BlockSpec((None, L, S5_WIDTH), lambda b, j: (b, nch - 1 - j, 0))
    yf = pl.pallas_call(
        functools.partial(_s5_kernel, rev=False, ntile=ntile), grid=(B, nch),
        in_specs=[fwd_blk] + [_const_spec(a.shape) for a in tabs_f],
        out_specs=fwd_blk, out_shape=jax.ShapeDtypeStruct((B, S, S5_WIDTH), F32),
        scratch_shapes=scratch, compiler_params=_cparams(("parallel", "arbitrary")), name="s5_fwd")(u, *tabs_f)
    rest = (d_skip, glu_w, glu_b, gain)
    return pl.pallas_call(
        functools.partial(_s5_kernel, rev=True, ntile=ntile), grid=(B, nch),
        in_specs=[bwd_blk, bwd_blk] + [_const_spec(a.shape) for a in tabs_b + rest],
        out_specs=bwd_blk, out_shape=jax.ShapeDtypeStruct((B, S, S5_WIDTH), BF),
        scratch_shapes=scratch, compiler_params=_cparams(("parallel", "arbitrary")), name="s5_bwd")(
            u, yf, *tabs_b, *rest)


GQA_ACC_ROWS = HEAD_DIM + 16


def _gqa_kernel(q_ref, k_ref, vt_ref, g_ref, o_ref, s_ref, p_ref, m_ref, a_ref, a2_ref, acc_ref, *, tk, nkv):
    q = q_ref[...]
    tq = q.shape[0]
    halves = (q[:, :128], q[:, 128:])
    left = lax.broadcasted_iota(jnp.int32, (tq, 128), 1) < HEAD_DIM
    zero = jnp.zeros((tq, 128), BF)
    qs = [jnp.where(left, halves[0], zero), jnp.where(left, zero, halves[0]),
          jnp.where(left, halves[1], zero), jnp.where(left, zero, halves[1])]
    m_ref[...] = jnp.full(m_ref.shape, -jnp.inf, F32)
    acc_ref[...] = jnp.zeros(acc_ref.shape, F32)
    ones_rows = jnp.ones((GQA_ACC_ROWS - HEAD_DIM, tk), BF)

    def scores(jb, slot):
        k = k_ref[pl.ds(pl.multiple_of(jb * tk, tk), tk), :]
        for i in range(4):
            st = _dot_nt(k, qs[i])
            s_ref[slot, i] = st
            m_old = m_ref[i:i + 1, :]
            m_new = jnp.maximum(m_old, jnp.max(st, axis=0, keepdims=True))
            a_ref[i:i + 1, :] = jnp.exp2(m_old - m_new)
            m_ref[i:i + 1, :] = m_new

    def probs(slot):
        for i in range(4):
            p_ref[slot, i] = jnp.exp2(s_ref[slot, i] - m_ref[i:i + 1, :]).astype(BF)
            a2_ref[i:i + 1, :] = a_ref[i:i + 1, :]

    def values(jb, slot):
        vt = vt_ref[:, pl.ds(pl.multiple_of(jb * tk, tk), tk)]
        lhs = [jnp.concatenate([vt[g * HEAD_DIM:(g + 1) * HEAD_DIM, :], ones_rows], axis=0)
               for g in range(GQA_KV_HEADS)]
        for i in range(4):
            pv = _dot(lhs[i % 2], p_ref[slot, i])
            acc_ref[i] = a2_ref[i:i + 1, :] * acc_ref[i] + pv

    scores(0, 0)
    probs(0)
    scores(1, 1)

    def kv_body(jj, carry):
        jb = 2 * jj
        values(jb, 0)
        probs(1)
        scores(jb + 2, 0)
        values(jb + 1, 1)
        probs(0)
        scores(jb + 3, 1)
        return carry

    lax.fori_loop(0, (nkv - 2) // 2, kv_body, 0)
    values(nkv - 2, (nkv - 2) % 2)
    probs((nkv - 1) % 2)
    values(nkv - 1, (nkv - 1) % 2)
    outs = [acc_ref[i, 0:HEAD_DIM, :] * (1.0 / acc_ref[i, HEAD_DIM:HEAD_DIM + 1, :]) for i in range(4)]
    ot = jnp.concatenate([outs[0], outs[2], outs[1], outs[3]], axis=0)
    o_ref[...] = _rms(ot.T, g_ref[...]).astype(BF)


def _gqa(qc, kc, vct, gain, tq, tk):
    B, S, _ = qc.shape
    assert S % (2 * tk) == 0 and S // tk >= 2, "the kv loop handles two blocks per trip"
    return pl.pallas_call(
        functools.partial(_gqa_kernel, tk=tk, nkv=S // tk), grid=(B, S // tq),
        in_specs=[pl.BlockSpec((None, tq, 256), lambda b, j: (b, j, 0)),
                  pl.BlockSpec((None, S, 128), lambda b, j: (b, 0, 0)),
                  pl.BlockSpec((None, 128, S), lambda b, j: (b, 0, 0)),
                  _const_spec(gain.shape)],
        out_specs=pl.BlockSpec((None, tq, MIXW), lambda b, j: (b, j, 0)),
        out_shape=jax.ShapeDtypeStruct((B, S, MIXW), BF),
        scratch_shapes=[pltpu.VMEM((2, 4, tk, tq), F32), pltpu.VMEM((2, 4, tk, tq), BF),
                        pltpu.VMEM((SUBLANES, tq), F32), pltpu.VMEM((SUBLANES, tq), F32),
                        pltpu.VMEM((SUBLANES, tq), F32), pltpu.VMEM((4, GQA_ACC_ROWS, tq), F32)],
        compiler_params=_cparams(("parallel", "parallel")), name="gqa")(qc, kc, vct, gain)


def _hgrn_level_masks(rev):
    t = np.arange(HG_CHUNK)[:, None]
    s = np.arange(HG_CHUNK)[None, :]
    out = []
    for m in HG_LEVELS:
        bt, bs = t // m, s // m
        ok = ((bt % 2 == 0) & (bs == bt + 1)) if rev else ((bt % 2 == 1) & (bs == bt - 1))
        out.append(np.tile(ok.astype(np.float32), (1, HGRN_HEADS)))
    for dlt in range(SUBLANES):
        ok = (s == (t + dlt if rev else t - dlt)) & (s // SUBLANES == t // SUBLANES)
        out.append(np.tile(ok.astype(np.float32), (1, HGRN_HEADS)))
    return np.stack(out)


def _hgrn_kernel(*refs, rev, nchunk):
    if rev:
        (q_ref, z_ref, v_ref, gate_ref, of_ref, lb_ref, tri_ref, ones_ref, lmask_ref, g_ref,
         o_ref, st_ref) = refs
    else:
        (q_ref, z_ref, v_ref, lb_ref, tri_ref, ones_ref, lmask_ref, o_ref, st_ref) = refs
    C = HG_CHUNK
    j = pl.program_id(1)

    @pl.when(j == 0)
    def _():
        st_ref[...] = jnp.zeros_like(st_ref)

    log_lb = lb_ref[0:1, :]
    log1m_lb = lb_ref[1:2, :]
    one_m_lb = lb_ref[2:3, :]
    tri = tri_ref[...]
    ones = ones_ref[...]
    row_in_tile = lax.broadcasted_iota(jnp.int32, (C, MIXW), 0) % SUBLANES
    bd_mask = (lax.broadcasted_iota(jnp.int32, (MIXW, MIXW), 0) // HEAD_DIM
               == lax.broadcasted_iota(jnp.int32, (MIXW, MIXW), 1) // HEAD_DIM)
    zero_row = jnp.zeros((1, MIXW), F32)

    def tile_roll(x, amt):
        return jnp.concatenate(
            [pltpu.roll(x[r:r + SUBLANES, :], amt, axis=0) for r in range(0, C, SUBLANES)], axis=0)

    def stack_heads(x):
        xb = x.astype(BF)
        return jnp.concatenate(
            [jnp.where(_head_mask(xb.shape, 1, h), xb, jnp.zeros_like(xb)) for h in range(HGRN_HEADS)], axis=0)

    def chunk_body(ci, carry):
        c = (nchunk - 1 - ci) if rev else ci
        rows = pl.ds(pl.multiple_of(c * C, C), C)
        q = q_ref[rows, :]
        z = z_ref[rows, :]
        v = v_ref[rows, :]
        ls = jnp.minimum(z, 0.0) - jnp.log1p(jnp.exp(-jnp.abs(z)))
        bv = log1m_lb + ls
        logf = jnp.maximum(log_lb, bv) + jnp.log1p(jnp.exp(-jnp.abs(log_lb - bv)))
        kin = one_m_lb * jnp.exp(ls - z)
        p1, p2, p3 = _split3(logf)
        b = _dot(tri, p1) + _dot(tri, p2) + _dot(tri, p3)
        edge = b[0:1, :] if rev else b[C - 1:C, :]

        st = st_ref[...]
        o = _dot_nt((q * jnp.exp(b)).astype(BF), st.astype(BF))

        att = jnp.zeros((C, HGRN_HEADS * C), F32)
        for li, m in enumerate(HG_LEVELS):
            nb = C // m
            qrefs, krefs = [], []
            for blk in range(nb):
                if rev:
                    qrow = (blk + 1) * m
                    krow = blk * m
                else:
                    qrow = blk * m - 1
                    krow = (blk + 1) * m - 1
                qr = zero_row if (qrow < 0 or qrow >= C) else b[qrow:qrow + 1, :]
                qrefs.append(jnp.broadcast_to(qr, (m, MIXW)))
                krefs.append(jnp.broadcast_to(b[krow:krow + 1, :], (m, MIXW)))
            bq = jnp.concatenate(qrefs, axis=0)
            bk = jnp.concatenate(krefs, axis=0)
            ql = (q * jnp.exp(b - bq)).astype(BF)
            kl = kin * jnp.exp(bk - b)
            att = att + lmask_ref[li] * _dot_nt(ql, stack_heads(kl))
        for dlt in range(SUBLANES):
            if dlt == 0:
                d = q * kin
            else:
                amt = SUBLANES - dlt if rev else dlt
                valid = (row_in_tile + dlt <= SUBLANES - 1) if rev else (row_in_tile >= dlt)
                ks = tile_roll(kin, amt)
                bs = tile_roll(b, amt)
                d = q * ks * jnp.exp(jnp.where(valid, b - bs, 0.0))
            att = att + lmask_ref[len(HG_LEVELS) + dlt] * _dot(d.astype(BF), ones)
        o = o + _dot(att.astype(BF), stack_heads(v))

        kb = (kin * jnp.exp(edge - b)).astype(BF)
        upd = _dot_tn(v.astype(BF), kb)
        st_ref[...] = st * jnp.exp(edge) + jnp.where(bd_mask, upd, 0.0)

        if rev:
            tot = o + of_ref[rows, :]
            gate = gate_ref[rows, :]
            silu = gate * (1.0 / (1.0 + jnp.exp(-gate)))
            o_ref[rows, :] = (_rms(tot, g_ref[...]) * silu).astype(BF)
        else:
            o_ref[rows, :] = o
        return carry

    lax.fori_loop(0, nchunk, chunk_body, 0, unroll=True)


def _hgrn(dd, lb_tab, gain, ones, tb):
    B, S, _ = dd.shape
    nblk = S // tb
    nchunk = tb // HG_CHUNK
    tri_f = jnp.asarray(np.tril(np.ones((HG_CHUNK, HG_CHUNK), np.float32)), BF)
    tri_b = jnp.asarray(np.triu(np.ones((HG_CHUNK, HG_CHUNK), np.float32)), BF)
    lm_f = jnp.asarray(_hgrn_level_masks(False))
    lm_b = jnp.asarray(_hgrn_level_masks(True))
    fcol = lambda col: pl.BlockSpec((None, tb, MIXW), lambda b, j: (b, j, col))
    bcol = lambda col: pl.BlockSpec((None, tb, MIXW), lambda b, j: (b, nblk - 1 - j, col))
    scratch = [pltpu.VMEM((MIXW, MIXW), F32)]
    consts_f = (lb_tab, tri_f, ones, lm_f)
    of = pl.pallas_call(
        functools.partial(_hgrn_kernel, rev=False, nchunk=nchunk), grid=(B, nblk),
        in_specs=[fcol(0), fcol(1), fcol(3)] + [_const_spec(a.shape) for a in consts_f],
        out_specs=fcol(0), out_shape=jax.ShapeDtypeStruct((B, S, MIXW), F32),
        scratch_shapes=scratch, compiler_params=_cparams(("parallel", "arbitrary")), name="hgrn_fwd")(
            dd, dd, dd, *consts_f)
    consts_b = (lb_tab, tri_b, ones, lm_b, gain)
    return pl.pallas_call(
        functools.partial(_hgrn_kernel, rev=True, nchunk=nchunk), grid=(B, nblk),
        in_specs=[bcol(0), bcol(2), bcol(3), bcol(4), bcol(0)] + [_const_spec(a.shape) for a in consts_b],
        out_specs=bcol(0), out_shape=jax.ShapeDtypeStruct((B, S, MIXW), BF),
        scratch_shapes=scratch, compiler_params=_cparams(("parallel", "arbitrary")), name="hgrn_bwd")(
            dd, dd, dd, dd, of, *consts_b)


def _mem_kv_kernel(mem_ref, nw_ref, wkv_ref, o_ref):
    o_ref[...] = _dot(_rms(mem_ref[...], nw_ref[...]).astype(BF), wkv_ref[...]).astype(BF)


def _mem_kv(mem, nw, wkv):
    B, M, _ = mem.shape
    return pl.pallas_call(
        _mem_kv_kernel, grid=(B,),
        in_specs=[pl.BlockSpec((None, M, D_MODEL), lambda b: (b, 0, 0)), _const_spec(nw.shape),
                  _const_spec(wkv.shape)],
        out_specs=pl.BlockSpec((None, M, 2 * MIXW), lambda b: (b, 0, 0)),
        out_shape=jax.ShapeDtypeStruct((B, M, 2 * MIXW), BF),
        compiler_params=_cparams(("parallel",)), name="mem_kv")(mem, nw, wkv)


def _post_kernel(x_ref, ma_ref, mb_ref, mc_ref, md_ref, wo_ref, nx_ref, wq_ref, kv_ref, wxo_ref, o_ref):
    x1 = x_ref[...]
    for i, m_ref in enumerate((ma_ref, mb_ref, mc_ref, md_ref)):
        x1 = x1 + _dot(m_ref[...], wo_ref[i * MIXW:(i + 1) * MIXW, :])
    hq = _rms(x1, nx_ref[...]).astype(BF)
    q = _dot(hq, wq_ref[...]).astype(BF)
    kmem = kv_ref[:, :MIXW]
    vmem = kv_ref[:, MIXW:]
    acc = jnp.zeros(q.shape, F32)
    for h in range(XA_HEADS):
        qm = jnp.where(_head_mask(q.shape, 1, h), q, jnp.zeros_like(q))
        s = _dot_nt(qm, kmem)
        m = jnp.max(s, axis=-1, keepdims=True)
        e = jnp.exp(s - m)
        p = (e * (1.0 / jnp.sum(e, axis=-1, keepdims=True))).astype(BF)
        vm = jnp.where(_head_mask(vmem.shape, 1, h), vmem, jnp.zeros_like(vmem))
        acc = acc + _dot(p, vm)
    o_ref[...] = x1 + _dot(acc.astype(BF), wxo_ref[...])


def _post(x, ma, mb, mc, md, kv, p, tm):
    B, S, _ = x.shape
    tok = lambda w: pl.BlockSpec((None, tm, w), lambda b, j: (b, j, 0))
    consts = (p["w_out"], p["norm_xattn_w"], p["xq"])
    return pl.pallas_call(
        _post_kernel, grid=(B, S // tm),
        in_specs=[tok(D_MODEL), tok(MIXW), tok(MIXW), tok(MIXW), tok(MIXW)]
        + [_const_spec(a.shape) for a in consts]
        + [pl.BlockSpec((None, MEM_TOKENS, 2 * MIXW), lambda b, j: (b, 0, 0)), _const_spec(p["xo"].shape)],
        out_specs=tok(D_MODEL), out_shape=jax.ShapeDtypeStruct(x.shape, F32),
        compiler_params=_cparams(("parallel", "parallel")), name="post")(
            x, ma, mb, mc, md, *consts, kv, p["xo"])


def _ffn_kernel(xp_ref, xc_ref, xn_ref, nw_ref, wa_ref, wg_ref, cw_ref, cb_ref, wd_ref, fw_ref, o_ref,
                hn_ref, gx_ref, a_ref, acc_ref, act_ref, *, tm, final):
    j = pl.program_id(1)
    nj = pl.num_programs(1)
    nw = nw_ref[...]
    hp = _rms(xp_ref[...], nw)
    hnx = _rms(xn_ref[...], nw)
    hn_ref[0:HALO, :] = jnp.where(j > 0, hp, 0.0).astype(BF)
    hn_ref[HALO:HALO + tm, :] = _rms(xc_ref[...], nw).astype(BF)
    hn_ref[HALO + tm:2 * HALO + tm, :] = jnp.where(j < nj - 1, hnx, 0.0).astype(BF)
    acc_ref[...] = jnp.zeros_like(acc_ref)

    def up(c, slot):
        a_ref[slot] = _dot(hn_ref[HALO:HALO + tm, :], wa_ref[c])
        gx_ref[slot] = _dot(hn_ref[...], wg_ref[c])

    def activate(c, slot):
        cw = cw_ref[c]
        gc = (cb_ref[c] + gx_ref[slot, HALO - 1:HALO - 1 + tm, :] * cw[0:1, :]
              + gx_ref[slot, HALO:HALO + tm, :] * cw[1:2, :]
              + gx_ref[slot, HALO + 1:HALO + 1 + tm, :] * cw[2:3, :])
        act_ref[slot] = (gc * (1.0 / (1.0 + jnp.exp(-gc))) * a_ref[slot]).astype(BF)

    def down(c, slot):
        acc_ref[...] += _dot(act_ref[slot], wd_ref[c])

    up(0, 0)
    up(1, 1)
    activate(0, 0)

    for c in range(FF_NCHUNK - 2):
        down(c, c % 2)
        activate(c + 1, (c + 1) % 2)
        up(c + 2, c % 2)
    down(FF_NCHUNK - 2, (FF_NCHUNK - 2) % 2)
    activate(FF_NCHUNK - 1, (FF_NCHUNK - 1) % 2)
    down(FF_NCHUNK - 1, (FF_NCHUNK - 1) % 2)
    y = xc_ref[...] + acc_ref[...]
    if final:
        y = _rms(y, fw_ref[...])
    o_ref[...] = y


def _ffn(x, p, final_w, tm, final):
    B, S, _ = x.shape
    per = tm // HALO
    nh = S // HALO
    consts = (p["norm_ffn_w"], p["up_a"], p["up_g"], p["conv_w"], p["conv_b"], p["down"], final_w)
    return pl.pallas_call(
        functools.partial(_ffn_kernel, tm=tm, final=final), grid=(B, S // tm),
        in_specs=[pl.BlockSpec((None, HALO, D_MODEL), lambda b, j: (b, jnp.maximum(j * per - 1, 0), 0)),
                  pl.BlockSpec((None, tm, D_MODEL), lambda b, j: (b, j, 0)),
                  pl.BlockSpec((None, HALO, D_MODEL), lambda b, j: (b, jnp.minimum((j + 1) * per, nh - 1), 0))]
        + [_const_spec(a.shape) for a in consts],
        out_specs=pl.BlockSpec((None, tm, D_MODEL), lambda b, j: (b, j, 0)),
        out_shape=jax.ShapeDtypeStruct(x.shape, F32),
        scratch_shapes=[pltpu.VMEM((tm + 2 * HALO, D_MODEL), BF), pltpu.VMEM((2, tm + 2 * HALO, FF_CHUNK), F32),
                        pltpu.VMEM((2, tm, FF_CHUNK), F32), pltpu.VMEM((tm, D_MODEL), F32),
                        pltpu.VMEM((2, tm, FF_CHUNK), BF)],
        compiler_params=_cparams(("parallel", "parallel")), name="ffn")(x, x, x, *consts)


def _rope_tables(S):
    t = jnp.arange(S)
    inv = 1.0 / (ROPE_THETA ** (jnp.arange(0, ROPE_AXIS_DIM, 2, dtype=F32) / ROPE_AXIS_DIM))
    ang = jnp.concatenate([(t // GRID_W).astype(F32)[:, None] * inv,
                           (t % GRID_W).astype(F32)[:, None] * inv], axis=-1)
    cos, sin = jnp.cos(ang), jnp.sin(ang)
    cos_h = jnp.concatenate([cos, cos], axis=-1)
    sin_h = jnp.concatenate([-sin, sin], axis=-1)
    return jnp.tile(cos_h, (1, GQA_Q_HEADS)), jnp.tile(sin_h, (1, GQA_Q_HEADS))


def _layer_params(l, w):
    row = lambda a: a.reshape(1, -1).astype(F32)
    w_in = w["w_in"][l]
    qscale = 1.0 / math.sqrt(HEAD_DIM)
    wa = jnp.concatenate([w_in[:, :256] * (qscale * LOG2E), w_in[:, 256:768]], axis=1)
    wq = w_in[:, 1024:1280].reshape(D_MODEL, GQA_Q_HEADS, HEAD_DIM)[:, np.array([0, 2, 1, 3]), :].reshape(
        D_MODEL, 256)
    blk = np.arange(256) // HEAD_DIM
    p = {
        "norm_mix_w": row(w["norm_mix_w"][l]),
        "wa": wa.astype(BF),
        "wb": w_in[:, 768:1024].astype(BF),
        "wq": wq.astype(BF),
        "wk": w_in[:, 1280:1408].astype(BF),
        "wvt": w_in[:, 1408:1536].T.astype(BF),
        "wd": w_in[:, 1536:].astype(BF),
        "ones": jnp.asarray(blk[:, None] == blk[None, :], BF),
        "gq": row(jnp.tile(w["gqa_q_norm_w"][l], GQA_Q_HEADS)),
        "gk": row(jnp.tile(w["gqa_k_norm_w"][l], GQA_Q_HEADS)),
        "na_bias": _na_bias_table(w["na_rpb"][l].astype(F32)),
        "s5_f": _s5_tables(*[w[k][l, 0] for k in ("s5_lambda_re", "s5_lambda_im", "s5_log_dt", "s5_b_re",
                                                    "s5_b_im", "s5_c_re", "s5_c_im")], rev=False),
        "s5_b": _s5_tables(*[w[k][l, 1] for k in ("s5_lambda_re", "s5_lambda_im", "s5_log_dt", "s5_b_re",
                                                    "s5_b_im", "s5_c_re", "s5_c_im")], rev=True),
        "s5_d": row(w["s5_d"][l]),
        "s5_glu_w": w["s5_glu_w"][l].astype(BF),
        "s5_glu_b": row(w["s5_glu_b"][l]),
        "w_out": w["w_out"][l].astype(BF),
        "norm_xattn_w": row(w["norm_xattn_w"][l]),
        "norm_mem_w": row(w["norm_mem_w"][l]),
        "xq": (w["xattn_w_q"][l] * qscale).astype(BF),
        "xkv": w["xattn_w_kv"][l].astype(BF),
        "xo": w["xattn_w_o"][l].astype(BF),
        "norm_ffn_w": row(w["norm_ffn_w"][l]),
        "up_a": w["ffn_w_up"][l][:, :D_FF].reshape(D_MODEL, FF_NCHUNK, FF_CHUNK).transpose(1, 0, 2).astype(BF),
        "up_g": w["ffn_w_up"][l][:, D_FF:].reshape(D_MODEL, FF_NCHUNK, FF_CHUNK).transpose(1, 0, 2).astype(BF),
        "conv_w": w["ffn_conv_w"][l].astype(F32).reshape(3, FF_NCHUNK, FF_CHUNK).transpose(1, 0, 2),
        "conv_b": w["ffn_conv_b"][l].astype(F32).reshape(FF_NCHUNK, 1, FF_CHUNK),
        "down": w["ffn_w_down"][l].reshape(FF_NCHUNK, FF_CHUNK, D_MODEL).astype(BF),
    }
    g = w["mix_out_norm_w"][l]
    p["g_a"], p["g_b"], p["g_c"], p["g_d"] = (row(g[i * MIXW:(i + 1) * MIXW]) for i in range(4))
    sm = jax.nn.softmax(w["hgrn_lower_bound"].astype(F32), axis=0)
    lb = jnp.concatenate([jnp.zeros_like(sm[:1]), jnp.cumsum(sm, axis=0)[:-1]], axis=0)[l]
    p["lb_tab"] = jnp.concatenate(
        [row(jnp.log(lb)), row(jnp.log1p(-lb)), row(1.0 - lb), jnp.zeros((SUBLANES - 3, MIXW), F32)], axis=0)
    return p


def _tiles(S):
    return {"tm": 512, "s5_l": 256, "tq": 256, "tk": 512, "hg_tb": 512}


def _encode(x, mem, layers, final_w):
    B, S, _ = x.shape
    t = _tiles(S)
    cos, sin = _rope_tables(S)
    depth = len(layers)
    for l, p in enumerate(layers):
        pp = dict(p, cos=cos, sin=sin)
        qkva, ub, qc, kc, vct, dd = _in_proj(x, pp, t["tm"])
        ma = _na(qkva, p["na_bias"], p["g_a"])
        mb = _s5(ub, p["s5_f"], p["s5_b"], p["s5_d"], p["s5_glu_w"], p["s5_glu_b"], p["g_b"], t["s5_l"])
        mc = _gqa(qc, kc, vct, p["g_c"], t["tq"], t["tk"])
        md = _hgrn(dd, p["lb_tab"], p["g_d"], p["ones"], t["hg_tb"])
        kv = _mem_kv(mem, p["norm_mem_w"], p["xkv"])
        x = _post(x, ma, mb, mc, md, kv, p, t["tm"])
        x = _ffn(x, p, final_w, t["tm"], final=(l == depth - 1))
    return x


def kernel(x_prompt, x_sample, mem_prompt, mem_sample, norm_mix_w, w_in, na_rpb, s5_lambda_re, s5_lambda_im, s5_log_dt, s5_b_re, s5_b_im, s5_c_re, s5_c_im, s5_d, s5_glu_w, s5_glu_b, gqa_q_norm_w, gqa_k_norm_w, hgrn_lower_bound, mix_out_norm_w, w_out, norm_xattn_w, norm_mem_w, xattn_w_q, xattn_w_kv, xattn_w_o, norm_ffn_w, ffn_w_up, ffn_conv_w, ffn_conv_b, ffn_w_down, final_norm_w):
    w = dict(norm_mix_w=norm_mix_w, w_in=w_in, na_rpb=na_rpb, s5_lambda_re=s5_lambda_re,
             s5_lambda_im=s5_lambda_im, s5_log_dt=s5_log_dt, s5_b_re=s5_b_re, s5_b_im=s5_b_im,
             s5_c_re=s5_c_re, s5_c_im=s5_c_im, s5_d=s5_d, s5_glu_w=s5_glu_w, s5_glu_b=s5_glu_b,
             gqa_q_norm_w=gqa_q_norm_w, gqa_k_norm_w=gqa_k_norm_w, hgrn_lower_bound=hgrn_lower_bound,
             mix_out_norm_w=mix_out_norm_w, w_out=w_out, norm_xattn_w=norm_xattn_w, norm_mem_w=norm_mem_w,
             xattn_w_q=xattn_w_q, xattn_w_kv=xattn_w_kv, xattn_w_o=xattn_w_o, norm_ffn_w=norm_ffn_w,
             ffn_w_up=ffn_w_up, ffn_conv_w=ffn_conv_w, ffn_conv_b=ffn_conv_b, ffn_w_down=ffn_w_down)
    layers = [_layer_params(l, w) for l in range(w_in.shape[0])]
    final_w = final_norm_w.reshape(1, -1).astype(F32)
    return (_encode(x_prompt, mem_prompt, layers, final_w), _encode(x_sample, mem_sample, layers, final_w))
```

```python
import functools
import math

import numpy as np
import jax
import jax.numpy as jnp
from jax import lax
from jax.experimental import pallas as pl
from jax.experimental.pallas import tpu as pltpu

D_MODEL = 1024
GRID_W = 64
HEAD_DIM = 64
EPS = 1e-6
NA_HEADS = 4
NA_WIN_ROWS = 8
NA_WIN_COLS = 16
S5_WIDTH = 256
S5_GROUP = 16
S5_GROUPS = 16
S5_STATE = 64
S5_NSTATE = S5_GROUPS * S5_STATE
GQA_Q_HEADS = 4
GQA_KV_HEADS = 2
ROPE_THETA = 10000.0
ROPE_AXIS_DIM = HEAD_DIM // 2
HGRN_HEADS = 4
XA_HEADS = 4
MEM_TOKENS = 256
D_FF = 2816
MIXW = 256
NEG = -1e30
LOG2E = 1.4426950408889634

SUBLANES = 8
FF_CHUNK = 256
FF_NCHUNK = D_FF // FF_CHUNK
HALO = 16
HG_CHUNK = 64
HG_LEVELS = (32, 16, 8)
VMEM_LIMIT = 56 * 1024 * 1024

BF = jnp.bfloat16
F32 = jnp.float32


def _cparams(sem):
    return pltpu.CompilerParams(dimension_semantics=sem, vmem_limit_bytes=VMEM_LIMIT)


def _dot(a, b):
    return jnp.dot(a, b, preferred_element_type=F32)


def _dot_nt(a, b):
    return lax.dot_general(a, b, (((1,), (1,)), ((), ())), preferred_element_type=F32)


def _dot_tn(a, b):
    return lax.dot_general(a, b, (((0,), (0,)), ((), ())), preferred_element_type=F32)


def _split3(x):
    h1 = x.astype(BF)
    r1 = x - h1.astype(F32)
    h2 = r1.astype(BF)
    r2 = r1 - h2.astype(F32)
    return h1, h2, r2.astype(BF)


def _rms(x, w):
    ms = jnp.mean(x * x, axis=-1, keepdims=True)
    return x * lax.rsqrt(ms + EPS) * w


def _head_mask(shape, axis, h):
    idx = lax.broadcasted_iota(jnp.int32, shape, axis)
    return (idx >= h * HEAD_DIM) & (idx < (h + 1) * HEAD_DIM)


def _const_spec(shape):
    nd = len(shape)
    return pl.BlockSpec(shape, lambda *_: (0,) * nd)


def _in_proj_kernel(x_ref, nw_ref, wa_ref, wb_ref, wq_ref, wk_ref, wvt_ref, wd_ref, ones_ref,
                    gq_ref, gk_ref, cos_ref, sin_ref,
                    qkva_ref, ub_ref, qc_ref, kc_ref, vct_ref, dd_ref):
    x = x_ref[...]
    h = _rms(x, nw_ref[...]).astype(BF)
    qkva_ref[...] = _dot(h, wa_ref[...]).astype(BF)
    ub_ref[...] = _dot(h, wb_ref[...])
    dd_ref[...] = _dot(h, wd_ref[...])
    vct_ref[...] = _dot_nt(wvt_ref[...], h).astype(BF)

    ones = ones_ref[...]
    cos = cos_ref[...]
    sin = sin_ref[...]

    def norm_rope(y, g, width):
        yy = y * y
        hi = yy.astype(BF)
        lo = (yy - hi.astype(F32)).astype(BF)
        ms = (_dot(hi, ones[:width, :width]) + _dot(lo, ones[:width, :width])) * (1.0 / HEAD_DIM)
        yn = y * lax.rsqrt(ms + EPS) * g
        lane = lax.broadcasted_iota(jnp.int32, yn.shape, 1)
        first_half = (lane % HEAD_DIM) < ROPE_AXIS_DIM
        swapped = jnp.where(first_half,
                            pltpu.roll(yn, width - ROPE_AXIS_DIM, axis=1),
                            pltpu.roll(yn, ROPE_AXIS_DIM, axis=1))
        return yn * cos[:, :width] + swapped * sin[:, :width]

    q = norm_rope(_dot(h, wq_ref[...]), gq_ref[...], 4 * HEAD_DIM)
    qc_ref[...] = (q * (LOG2E / math.sqrt(HEAD_DIM))).astype(BF)
    k = norm_rope(_dot(h, wk_ref[...]), gk_ref[...][:, :2 * HEAD_DIM], 2 * HEAD_DIM)
    kc_ref[...] = k.astype(BF)


def _batch_rows_spec(tm):
    return pl.BlockSpec((None, tm, MIXW), lambda b, j: (b // SUBLANES, j, b % SUBLANES))


def _in_proj(x, p, tm, ub_batch_rows):
    B, S, _ = x.shape
    grid = (B, S // tm)
    tok = lambda w: pl.BlockSpec((None, tm, w), lambda b, j: (b, j, 0))
    ub_shape = (B // SUBLANES, S, SUBLANES * MIXW) if ub_batch_rows else (B, S, MIXW)
    ub_spec = _batch_rows_spec(tm) if ub_batch_rows else tok(MIXW)
    tab = pl.BlockSpec((tm, 4 * HEAD_DIM), lambda b, j: (j, 0))
    ins = [x, p["norm_mix_w"], p["wa"], p["wb"], p["wq"], p["wk"], p["wvt"], p["wd"], p["ones"],
           p["gq"], p["gk"], p["cos"], p["sin"]]
    in_specs = [tok(D_MODEL)] + [_const_spec(a.shape) for a in ins[1:11]] + [tab, tab]
    out_shape = (
        jax.ShapeDtypeStruct((B, S, 768), BF),
        jax.ShapeDtypeStruct(ub_shape, F32),
        jax.ShapeDtypeStruct((B, S, 256), BF),
        jax.ShapeDtypeStruct((B, S, 128), BF),
        jax.ShapeDtypeStruct((B, 128, S), BF),
        jax.ShapeDtypeStruct((B, S, 1280), F32),
    )
    out_specs = (tok(768), ub_spec, tok(256), tok(128),
                 pl.BlockSpec((None, 128, tm), lambda b, j: (b, 0, j)), tok(1280))
    return pl.pallas_call(
        _in_proj_kernel, grid=grid, in_specs=in_specs, out_specs=out_specs, out_shape=out_shape,
        compiler_params=_cparams(("parallel", "parallel")), name="in_proj")(*ins)


NA_CHUNK_ROWS = 8
NA_CHUNK = NA_CHUNK_ROWS * GRID_W
NA_KEYS = NA_WIN_ROWS * GRID_W


def _na_kernel(q_ref, kp_ref, kc_ref, kn_ref, vp_ref, vc_ref, vn_ref, bias_ref, g_ref, o_ref,
               kbuf, vbuf, *, rows):
    j = pl.program_id(1)
    kbuf[0:NA_CHUNK, :] = kp_ref[...]
    kbuf[NA_CHUNK:2 * NA_CHUNK, :] = kc_ref[...]
    kbuf[2 * NA_CHUNK:3 * NA_CHUNK, :] = kn_ref[...]
    for part, ref in enumerate((vp_ref, vc_ref, vn_ref)):
        v = ref[...]
        for h in range(NA_HEADS):
            vbuf[h, part * NA_CHUNK:(part + 1) * NA_CHUNK, :] = jnp.where(
                _head_mask(v.shape, 1, h), v, jnp.zeros_like(v))
    gain = g_ref[...]

    def row_body(rr, carry):
        r = j * NA_CHUNK_ROWS + rr
        r0 = jnp.clip(r - NA_WIN_ROWS // 2, 0, rows - NA_WIN_ROWS)
        off = pl.multiple_of((r0 - j * NA_CHUNK_ROWS + NA_CHUNK_ROWS) * GRID_W, GRID_W)
        variant = r - r0
        qrow = pl.multiple_of(rr * GRID_W, GRID_W)
        q = q_ref[pl.ds(qrow, GRID_W), :]
        kw = kbuf[pl.ds(off, NA_KEYS), :]
        qstack = jnp.concatenate(
            [jnp.where(_head_mask(q.shape, 1, h), q, jnp.zeros_like(q)) for h in range(NA_HEADS)], axis=0)
        s = _dot_nt(qstack, kw) + bias_ref[variant]
        m = jnp.max(s, axis=-1, keepdims=True)
        e = jnp.exp2(s - m)
        p = (e * (1.0 / jnp.sum(e, axis=-1, keepdims=True))).astype(BF)
        acc = _dot(p[0:GRID_W], vbuf[0, pl.ds(off, NA_KEYS), :])
        for h in range(1, NA_HEADS):
            acc = acc + _dot(p[h * GRID_W:(h + 1) * GRID_W], vbuf[h, pl.ds(off, NA_KEYS), :])
        o_ref[pl.ds(qrow, GRID_W), :] = _rms(acc, gain).astype(BF)
        return carry

    lax.fori_loop(0, NA_CHUNK_ROWS, row_body, 0, unroll=True)


def _na_bias_table(rpb):
    qcols = np.arange(GRID_W)
    kcols = np.arange(GRID_W)
    c0 = np.clip(qcols - NA_WIN_COLS // 2, 0, GRID_W - NA_WIN_COLS)
    in_win = (kcols[None, :] >= c0[:, None]) & (kcols[None, :] < c0[:, None] + NA_WIN_COLS)
    dc = np.clip(kcols[None, :] - qcols[:, None], -(NA_WIN_COLS - 1), NA_WIN_COLS - 1) + (NA_WIN_COLS - 1)
    onehot = jnp.asarray(dc[None, :, :] == np.arange(2 * NA_WIN_COLS - 1)[:, None, None], F32)
    cols = jnp.einsum('hrc,cqk->hrqk', rpb, onehot, precision=lax.Precision.HIGHEST) * LOG2E
    cols = jnp.where(in_win[None, None], cols, NEG)
    variants = []
    for var in range(NA_WIN_ROWS):
        lo = NA_WIN_ROWS - 1 - var
        bias = cols[:, lo:lo + NA_WIN_ROWS]
        variants.append(bias.transpose(0, 2, 1, 3).reshape(NA_HEADS * GRID_W, NA_KEYS))
    return jnp.stack(variants)


def _na(qkva, bias, gain):
    B, S, _ = qkva.shape
    rows = S // GRID_W
    nchunk = S // NA_CHUNK
    blk = lambda col, f: pl.BlockSpec((None, NA_CHUNK, MIXW), lambda b, j: (b, f(j), col))
    prev = lambda j: jnp.maximum(j - 1, 0)
    cur = lambda j: j
    nxt = lambda j: jnp.minimum(j + 1, nchunk - 1)
    in_specs = [blk(0, cur), blk(1, prev), blk(1, cur), blk(1, nxt), blk(2, prev), blk(2, cur), blk(2, nxt),
                _const_spec(bias.shape), _const_spec(gain.shape)]
    return pl.pallas_call(
        functools.partial(_na_kernel, rows=rows), grid=(B, nchunk), in_specs=in_specs,
        out_specs=pl.BlockSpec((None, NA_CHUNK, MIXW), lambda b, j: (b, j, 0)),
        out_shape=jax.ShapeDtypeStruct((B, S, MIXW), BF),
        scratch_shapes=[pltpu.VMEM((3 * NA_CHUNK, MIXW), BF), pltpu.VMEM((NA_HEADS, 3 * NA_CHUNK, MIXW), BF)],
        compiler_params=_cparams(("parallel", "parallel")), name="na")(
            qkva, qkva, qkva, qkva, qkva, qkva, qkva, bias, gain)


S5_LANES = 256


def _s5_kernel(*refs, rev, ntile, batch_rows):
    if rev:
        (u_ref, yf_ref, bbd_ref, cc_ref, tab_ref, d_ref, gw_ref, gb_ref, g_ref, o_ref, z_ref, carry_ref) = refs
    else:
        (u_ref, bbd_ref, cc_ref, tab_ref, o_ref, z_ref, carry_ref) = refs
    j = pl.program_id(1)

    @pl.when(j == 0)
    def _():
        carry_ref[...] = jnp.zeros_like(carry_ref)

    u = u_ref[...]
    ub = u.astype(BF)
    y = None
    for c0 in range(0, S5_NSTATE, S5_LANES):
        re = slice(c0, c0 + S5_LANES)
        im = slice(S5_NSTATE + c0, S5_NSTATE + c0 + S5_LANES)
        z_ref[:, re] = _dot(ub, bbd_ref[:, re])
        z_ref[:, im] = _dot(ub, bbd_ref[:, im])
        steps = [(tab_ref[2 * i, :, re], tab_ref[2 * i + 1, :, re], 1 << i) for i in range(3)]
        pr, pi = tab_ref[6, :, re], tab_ref[7, :, re]
        cr, ci = carry_ref[:, re], carry_ref[:, im]
        last = 0 if rev else SUBLANES - 1
        if batch_rows:
            first = SUBLANES - 1 - last
            pr = jnp.broadcast_to(pr[first:first + 1, :], pr.shape)
            pi = jnp.broadcast_to(pi[first:first + 1, :], pi.shape)
        for i in range(ntile):
            t = (ntile - 1 - i) if rev else i
            rows = slice(t * SUBLANES, (t + 1) * SUBLANES)
            zr = z_ref[rows, re]
            zi = z_ref[rows, im]
            if not batch_rows:
                for ar, ai, sh in steps:
                    amt = SUBLANES - sh if rev else sh
                    sr = pltpu.roll(zr, amt, axis=0)
                    si = pltpu.roll(zi, amt, axis=0)
                    zr, zi = zr + ar * sr - ai * si, zi + ar * si + ai * sr
            zr, zi = zr + pr * cr - pi * ci, zi + pr * ci + pi * cr
            z_ref[rows, re] = zr
            z_ref[rows, im] = zi
            if batch_rows:
                cr, ci = zr, zi
            else:
                cr = jnp.broadcast_to(zr[last:last + 1, :], zr.shape)
                ci = jnp.broadcast_to(zi[last:last + 1, :], zi.shape)
        carry_ref[:, re] = cr
        carry_ref[:, im] = ci
        part = _dot(z_ref[:, re].astype(BF), cc_ref[re, :]) + _dot(z_ref[:, im].astype(BF), cc_ref[im, :])
        y = part if y is None else y + part
    if not rev:
        o_ref[...] = y
    else:
        y = y + yf_ref[...] + d_ref[...] * u
        hcur = 0.5 * y * (1.0 + jnp.tanh(math.sqrt(2.0 / math.pi) * (y + 0.044715 * (y * y * y))))
        gate = _dot(hcur.astype(BF), gw_ref[...]) + gb_ref[...]
        out = hcur * (1.0 / (1.0 + jnp.exp(-gate)))
        o_ref[...] = _rms(out, g_ref[...]).astype(BF)


def _s5_tables(lam_re, lam_im, log_dt, b_re, b_im, c_re, c_im, rev):
    lre = jnp.minimum(lam_re.astype(F32), -1e-4)
    lim = lam_im.astype(F32)
    dt = jnp.exp(log_dt.astype(F32))[:, None]
    mag = jnp.exp(lre * dt)
    ar = mag * jnp.cos(lim * dt)
    ai = mag * jnp.sin(lim * dt)
    den = lre * lre + lim * lim
    nre = ar - 1.0
    coef_re = (nre * lre + ai * lim) / den
    coef_im = (ai * lre - nre * lim) / den
    bre, bim = b_re.astype(F32), b_im.astype(F32)
    bbar_re = coef_re[..., None] * bre - coef_im[..., None] * bim
    bbar_im = coef_re[..., None] * bim + coef_im[..., None] * bre
    eye = jnp.eye(S5_GROUPS, dtype=F32)
    bd = lambda m: jnp.einsum('gnc,gh->gchn', m, eye).reshape(S5_WIDTH, S5_NSTATE)
    bbd = jnp.concatenate([bd(bbar_re), bd(bbar_im)], axis=1).astype(BF)
    cd = lambda m: jnp.einsum('gcn,gh->gnhc', m.astype(F32), eye).reshape(S5_NSTATE, S5_WIDTH)
    cc = jnp.concatenate([cd(c_re), -cd(c_im)], axis=0).astype(BF)
    a1r, a1i = ar.reshape(1, -1), ai.reshape(1, -1)
    cmul = lambda xr, xi, yr, yi: (xr * yr - xi * yi, xr * yi + xi * yr)
    a2r, a2i = cmul(a1r, a1i, a1r, a1i)
    a4r, a4i = cmul(a2r, a2i, a2r, a2i)
    pw = [(a1r, a1i)]
    for _ in range(SUBLANES - 1):
        pw.append(cmul(pw[-1][0], pw[-1][1], a1r, a1i))
    pr = jnp.concatenate([x[0] for x in pw], axis=0)
    pi = jnp.concatenate([x[1] for x in pw], axis=0)
    rows = np.arange(SUBLANES)[:, None]
    tabs = []
    for (xr, xi), sh in (((a1r, a1i), 1), ((a2r, a2i), 2), ((a4r, a4i), 4)):
        keep = jnp.asarray(rows >= sh, F32)
        tabs += [keep * xr, keep * xi]
    tabs += [pr, pi]
    tab = jnp.stack(tabs)
    if rev:
        tab = tab[:, ::-1, :]
    return bbd, cc, tab


def _s5(u, tabs_f, tabs_b, d_skip, glu_w, glu_b, gain, L, batch_rows):
    G, R, _ = u.shape
    nch = R // L
    ntile = L // SUBLANES
    scratch = [pltpu.VMEM((L, 2 * S5_NSTATE), F32), pltpu.VMEM((SUBLANES, 2 * S5_NSTATE), F32)]
    fwd_blk = pl.BlockSpec((None, L, S5_WIDTH), lambda b, j: (b, j, 0))
    bwd_blk = pl.BlockSpec((None, L, S5_WIDTH), lambda b, j: (b, nch - 1 - j, 0))
    kern = functools.partial(_s5_kernel, ntile=ntile, batch_rows=batch_rows)
    yf = pl.pallas_call(
        functools.partial(kern, rev=False), grid=(G, nch),
        in_specs=[fwd_blk] + [_const_spec(a.shape) for a in tabs_f],
        out_specs=fwd_blk, out_shape=jax.ShapeDtypeStruct((G, R, S5_WIDTH), F32),
        scratch_shapes=scratch, compiler_params=_cparams(("parallel", "arbitrary")), name="s5_fwd")(u, *tabs_f)
    rest = (d_skip, glu_w, glu_b, gain)
    return pl.pallas_call(
        functools.partial(kern, rev=True), grid=(G, nch),
        in_specs=[bwd_blk, bwd_blk] + [_const_spec(a.shape) for a in tabs_b + rest],
        out_specs=bwd_blk, out_shape=jax.ShapeDtypeStruct((G, R, S5_WIDTH), BF),
        scratch_shapes=scratch, compiler_params=_cparams(("parallel", "arbitrary")), name="s5_bwd")(
            u, yf, *tabs_b, *rest)


GQA_ACC_ROWS = HEAD_DIM + 16


def _gqa_kernel(q_ref, k_ref, vt_ref, g_ref, o_ref, s_ref, p_ref, m_ref, a_ref, a2_ref, acc_ref, *, tk, nkv):
    q = q_ref[...]
    tq = q.shape[0]
    halves = (q[:, :128], q[:, 128:])
    left = lax.broadcasted_iota(jnp.int32, (tq, 128), 1) < HEAD_DIM
    zero = jnp.zeros((tq, 128), BF)
    qs = [jnp.where(left, halves[0], zero), jnp.where(left, zero, halves[0]),
          jnp.where(left, halves[1], zero), jnp.where(left, zero, halves[1])]
    m_ref[...] = jnp.full(m_ref.shape, -jnp.inf, F32)
    acc_ref[...] = jnp.zeros(acc_ref.shape, F32)
    ones_rows = jnp.ones((GQA_ACC_ROWS - HEAD_DIM, tk), BF)

    def scores(jb, slot):
        k = k_ref[pl.ds(pl.multiple_of(jb * tk, tk), tk), :]
        for i in range(4):
            st = _dot_nt(k, qs[i])
            s_ref[slot, i] = st
            m_old = m_ref[i:i + 1, :]
            m_new = jnp.maximum(m_old, jnp.max(st, axis=0, keepdims=True))
            a_ref[i:i + 1, :] = jnp.exp2(m_old - m_new)
            m_ref[i:i + 1, :] = m_new

    def probs(slot):
        for i in range(4):
            p_ref[slot, i] = jnp.exp2(s_ref[slot, i] - m_ref[i:i + 1, :]).astype(BF)
            a2_ref[i:i + 1, :] = a_ref[i:i + 1, :]

    def values(jb, slot):
        vt = vt_ref[:, pl.ds(pl.multiple_of(jb * tk, tk), tk)]
        lhs = [jnp.concatenate([vt[g * HEAD_DIM:(g + 1) * HEAD_DIM, :], ones_rows], axis=0)
               for g in range(GQA_KV_HEADS)]
        for i in range(4):
            pv = _dot(lhs[i % 2], p_ref[slot, i])
            acc_ref[i] = a2_ref[i:i + 1, :] * acc_ref[i] + pv

    scores(0, 0)
    probs(0)
    scores(1, 1)

    def stage_triple(jb, slot):
        values(jb, slot)
        probs(1 - slot)
        scores(jb + 2, slot)

    def kv_body(jj, carry):
        stage_triple(2 * jj, 0)
        stage_triple(2 * jj + 1, 1)
        return carry

    lax.fori_loop(0, (nkv - 2) // 2, kv_body, 0)
    values(nkv - 2, (nkv - 2) % 2)
    probs((nkv - 1) % 2)
    values(nkv - 1, (nkv - 1) % 2)
    outs = [acc_ref[i, 0:HEAD_DIM, :] * (1.0 / acc_ref[i, HEAD_DIM:HEAD_DIM + 1, :]) for i in range(4)]
    ot = jnp.concatenate([outs[0], outs[2], outs[1], outs[3]], axis=0)
    o_ref[...] = _rms(ot.T, g_ref[...]).astype(BF)


def _gqa(qc, kc, vct, gain, tq, tk):
    B, S, _ = qc.shape
    assert S % (2 * tk) == 0 and S // tk >= 2, "the kv loop handles two blocks per trip"
    return pl.pallas_call(
        functools.partial(_gqa_kernel, tk=tk, nkv=S // tk), grid=(B, S // tq),
        in_specs=[pl.BlockSpec((None, tq, 256), lambda b, j: (b, j, 0)),
                  pl.BlockSpec((None, S, 128), lambda b, j: (b, 0, 0)),
                  pl.BlockSpec((None, 128, S), lambda b, j: (b, 0, 0)),
                  _const_spec(gain.shape)],
        out_specs=pl.BlockSpec((None, tq, MIXW), lambda b, j: (b, j, 0)),
        out_shape=jax.ShapeDtypeStruct((B, S, MIXW), BF),
        scratch_shapes=[pltpu.VMEM((2, 4, tk, tq), F32), pltpu.VMEM((2, 4, tk, tq), BF),
                        pltpu.VMEM((SUBLANES, tq), F32), pltpu.VMEM((SUBLANES, tq), F32),
                        pltpu.VMEM((SUBLANES, tq), F32), pltpu.VMEM((4, GQA_ACC_ROWS, tq), F32)],
        compiler_params=_cparams(("parallel", "parallel")), name="gqa")(qc, kc, vct, gain)


def _hgrn_level_masks(rev):
    t = np.arange(HG_CHUNK)[:, None]
    s = np.arange(HG_CHUNK)[None, :]
    out = []
    for m in HG_LEVELS:
        bt, bs = t // m, s // m
        ok = ((bt % 2 == 0) & (bs == bt + 1)) if rev else ((bt % 2 == 1) & (bs == bt - 1))
        out.append(np.tile(ok.astype(np.float32), (1, HGRN_HEADS)))
    for dlt in range(SUBLANES):
        ok = (s == (t + dlt if rev else t - dlt)) & (s // SUBLANES == t // SUBLANES)
        out.append(np.tile(ok.astype(np.float32), (1, HGRN_HEADS)))
    return np.stack(out)


def _hgrn_kernel(*refs, rev, nchunk):
    if rev:
        (q_ref, z_ref, v_ref, gate_ref, of_ref, lb_ref, tri_ref, ones_ref, lmask_ref, g_ref,
         o_ref, st_ref) = refs
    else:
        (q_ref, z_ref, v_ref, lb_ref, tri_ref, ones_ref, lmask_ref, o_ref, st_ref) = refs
    C = HG_CHUNK
    j = pl.program_id(1)

    @pl.when(j == 0)
    def _():
        st_ref[...] = jnp.zeros_like(st_ref)

    log_lb = lb_ref[0:1, :]
    log1m_lb = lb_ref[1:2, :]
    one_m_lb = lb_ref[2:3, :]
    tri = tri_ref[...]
    ones = ones_ref[...]
    row_in_tile = lax.broadcasted_iota(jnp.int32, (C, MIXW), 0) % SUBLANES
    bd_mask = (lax.broadcasted_iota(jnp.int32, (MIXW, MIXW), 0) // HEAD_DIM
               == lax.broadcasted_iota(jnp.int32, (MIXW, MIXW), 1) // HEAD_DIM)
    zero_row = jnp.zeros((1, MIXW), F32)

    def tile_roll(x, amt):
        return jnp.concatenate(
            [pltpu.roll(x[r:r + SUBLANES, :], amt, axis=0) for r in range(0, C, SUBLANES)], axis=0)

    def stack_heads(x):
        xb = x.astype(BF)
        return jnp.concatenate(
            [jnp.where(_head_mask(xb.shape, 1, h), xb, jnp.zeros_like(xb)) for h in range(HGRN_HEADS)], axis=0)

    def chunk_body(ci, carry):
        c = (nchunk - 1 - ci) if rev else ci
        rows = pl.ds(pl.multiple_of(c * C, C), C)
        q = q_ref[rows, :]
        z = z_ref[rows, :]
        v = v_ref[rows, :]
        ls = jnp.minimum(z, 0.0) - jnp.log(1.0 + jnp.exp(-jnp.abs(z)))
        bv = log1m_lb + ls
        logf = jnp.maximum(log_lb, bv) + jnp.log(1.0 + jnp.exp(-jnp.abs(log_lb - bv)))
        kin = one_m_lb * jnp.exp(ls - z)
        pieces = _dot(tri, jnp.concatenate(_split3(logf * LOG2E), axis=1))
        b = pieces[:, 0:MIXW] + pieces[:, MIXW:2 * MIXW] + pieces[:, 2 * MIXW:3 * MIXW]
        edge = b[0:1, :] if rev else b[C - 1:C, :]

        st = st_ref[...]
        o = _dot_nt((q * jnp.exp2(b)).astype(BF), st.astype(BF))

        att = jnp.zeros((C, HGRN_HEADS * C), F32)
        for li, m in enumerate(HG_LEVELS):
            nb = C // m
            qrefs, krefs = [], []
            for blk in range(nb):
                if rev:
                    qrow = (blk + 1) * m
                    krow = blk * m
                else:
                    qrow = blk * m - 1
                    krow = (blk + 1) * m - 1
                qr = zero_row if (qrow < 0 or qrow >= C) else b[qrow:qrow + 1, :]
                qrefs.append(jnp.broadcast_to(qr, (m, MIXW)))
                krefs.append(jnp.broadcast_to(b[krow:krow + 1, :], (m, MIXW)))
            bq = jnp.concatenate(qrefs, axis=0)
            bk = jnp.concatenate(krefs, axis=0)
            ql = (q * jnp.exp2(b - bq)).astype(BF)
            kl = kin * jnp.exp2(bk - b)
            att = att + lmask_ref[li] * _dot_nt(ql, stack_heads(kl))
        ds = [(q * kin).astype(BF)]
        for dlt in range(1, SUBLANES):
            amt = SUBLANES - dlt if rev else dlt
            valid = (row_in_tile + dlt <= SUBLANES - 1) if rev else (row_in_tile >= dlt)
            ks = tile_roll(kin, amt)
            bs = tile_roll(b, amt)
            ds.append((q * ks * jnp.exp2(jnp.where(valid, b - bs, 0.0))).astype(BF))
        hsum = _dot(jnp.concatenate(ds, axis=0), ones)
        for dlt in range(SUBLANES):
            att = att + lmask_ref[len(HG_LEVELS) + dlt] * hsum[dlt * C:(dlt + 1) * C, :]
        o = o + _dot(att.astype(BF), stack_heads(v))

        kb = (kin * jnp.exp2(edge - b)).astype(BF)
        upd = _dot_tn(v.astype(BF), kb)
        st_ref[...] = st * jnp.exp2(edge) + jnp.where(bd_mask, upd, 0.0)

        if rev:
            tot = o + of_ref[rows, :]
            gate = gate_ref[rows, :]
            silu = gate * (1.0 / (1.0 + jnp.exp(-gate)))
            o_ref[rows, :] = (_rms(tot, g_ref[...]) * silu).astype(BF)
        else:
            o_ref[rows, :] = o
        return carry

    lax.fori_loop(0, nchunk, chunk_body, 0, unroll=True)


def _hgrn(dd, lb_tab, gain, ones, tb):
    B, S, _ = dd.shape
    nblk = S // tb
    nchunk = tb // HG_CHUNK
    tri_f = jnp.asarray(np.tril(np.ones((HG_CHUNK, HG_CHUNK), np.float32)), BF)
    tri_b = jnp.asarray(np.triu(np.ones((HG_CHUNK, HG_CHUNK), np.float32)), BF)
    lm_f = jnp.asarray(_hgrn_level_masks(False))
    lm_b = jnp.asarray(_hgrn_level_masks(True))
    fcol = lambda col: pl.BlockSpec((None, tb, MIXW), lambda b, j: (b, j, col))
    bcol = lambda col: pl.BlockSpec((None, tb, MIXW), lambda b, j: (b, nblk - 1 - j, col))
    scratch = [pltpu.VMEM((MIXW, MIXW), F32)]
    consts_f = (lb_tab, tri_f, ones, lm_f)
    of = pl.pallas_call(
        functools.partial(_hgrn_kernel, rev=False, nchunk=nchunk), grid=(B, nblk),
        in_specs=[fcol(0), fcol(1), fcol(3)] + [_const_spec(a.shape) for a in consts_f],
        out_specs=fcol(0), out_shape=jax.ShapeDtypeStruct((B, S, MIXW), F32),
        scratch_shapes=scratch, compiler_params=_cparams(("parallel", "arbitrary")), name="hgrn_fwd")(
            dd, dd, dd, *consts_f)
    consts_b = (lb_tab, tri_b, ones, lm_b, gain)
    return pl.pallas_call(
        functools.partial(_hgrn_kernel, rev=True, nchunk=nchunk), grid=(B, nblk),
        in_specs=[bcol(0), bcol(2), bcol(3), bcol(4), bcol(0)] + [_const_spec(a.shape) for a in consts_b],
        out_specs=bcol(0), out_shape=jax.ShapeDtypeStruct((B, S, MIXW), BF),
        scratch_shapes=scratch, compiler_params=_cparams(("parallel", "arbitrary")), name="hgrn_bwd")(
            dd, dd, dd, dd, of, *consts_b)


def _mem_kv_kernel(mem_ref, nw_ref, wkv_ref, o_ref):
    o_ref[...] = _dot(_rms(mem_ref[...], nw_ref[...]).astype(BF), wkv_ref[...]).astype(BF)


def _mem_kv(mem, nw, wkv):
    B, M, _ = mem.shape
    return pl.pallas_call(
        _mem_kv_kernel, grid=(B,),
        in_specs=[pl.BlockSpec((None, M, D_MODEL), lambda b: (b, 0, 0)), _const_spec(nw.shape),
                  _const_spec(wkv.shape)],
        out_specs=pl.BlockSpec((None, M, 2 * MIXW), lambda b: (b, 0, 0)),
        out_shape=jax.ShapeDtypeStruct((B, M, 2 * MIXW), BF),
        compiler_params=_cparams(("parallel",)), name="mem_kv")(mem, nw, wkv)


def _post_kernel(x_ref, ma_ref, mb_ref, mc_ref, md_ref, wo_ref, nx_ref, wq_ref, kv_ref, wxo_ref, o_ref):
    x1 = x_ref[...]
    for i, m_ref in enumerate((ma_ref, mb_ref, mc_ref, md_ref)):
        x1 = x1 + _dot(m_ref[...], wo_ref[i * MIXW:(i + 1) * MIXW, :])
    hq = _rms(x1, nx_ref[...]).astype(BF)
    q = _dot(hq, wq_ref[...]).astype(BF)
    kmem = kv_ref[:, :MIXW]
    vmem = kv_ref[:, MIXW:]
    acc = jnp.zeros(q.shape, F32)
    for h in range(XA_HEADS):
        qm = jnp.where(_head_mask(q.shape, 1, h), q, jnp.zeros_like(q))
        s = _dot_nt(qm, kmem)
        m = jnp.max(s, axis=-1, keepdims=True)
        e = jnp.exp(s - m)
        p = (e * (1.0 / jnp.sum(e, axis=-1, keepdims=True))).astype(BF)
        vm = jnp.where(_head_mask(vmem.shape, 1, h), vmem, jnp.zeros_like(vmem))
        acc = acc + _dot(p, vm)
    o_ref[...] = x1 + _dot(acc.astype(BF), wxo_ref[...])


def _post(x, ma, mb, mc, md, kv, p, tm):
    B, S, _ = x.shape
    tok = lambda w: pl.BlockSpec((None, tm, w), lambda b, j: (b, j, 0))
    consts = (p["w_out"], p["norm_xattn_w"], p["xq"])
    mb_spec = tok(MIXW) if mb.shape == ma.shape else _batch_rows_spec(tm)
    return pl.pallas_call(
        _post_kernel, grid=(B, S // tm),
        in_specs=[tok(D_MODEL), tok(MIXW), mb_spec, tok(MIXW), tok(MIXW)]
        + [_const_spec(a.shape) for a in consts]
        + [pl.BlockSpec((None, MEM_TOKENS, 2 * MIXW), lambda b, j: (b, 0, 0)), _const_spec(p["xo"].shape)],
        out_specs=tok(D_MODEL), out_shape=jax.ShapeDtypeStruct(x.shape, F32),
        compiler_params=_cparams(("parallel", "parallel")), name="post")(
            x, ma, mb, mc, md, *consts, kv, p["xo"])


def _ffn_kernel(xp_ref, xc_ref, xn_ref, nw_ref, wa_ref, wg_ref, cw_ref, cb_ref, wd_ref, fw_ref, o_ref,
                hn_ref, gx_ref, a_ref, acc_ref, act_ref, *, tm, final):
    j = pl.program_id(1)
    nj = pl.num_programs(1)
    nw = nw_ref[...]
    hp = _rms(xp_ref[...], nw)
    hnx = _rms(xn_ref[...], nw)
    hn_ref[0:HALO, :] = jnp.where(j > 0, hp, 0.0).astype(BF)
    hn_ref[HALO:HALO + tm, :] = _rms(xc_ref[...], nw).astype(BF)
    hn_ref[HALO + tm:2 * HALO + tm, :] = jnp.where(j < nj - 1, hnx, 0.0).astype(BF)
    acc_ref[...] = jnp.zeros_like(acc_ref)

    def up(c, slot):
        a_ref[slot] = _dot(hn_ref[HALO:HALO + tm, :], wa_ref[c])
        gx_ref[slot] = _dot(hn_ref[...], wg_ref[c])

    def activate(c, slot):
        cw = cw_ref[c]
        gc = (cb_ref[c] + gx_ref[slot, HALO - 1:HALO - 1 + tm, :] * cw[0:1, :]
              + gx_ref[slot, HALO:HALO + tm, :] * cw[1:2, :]
              + gx_ref[slot, HALO + 1:HALO + 1 + tm, :] * cw[2:3, :])
        act_ref[slot] = (gc * (1.0 / (1.0 + jnp.exp(-gc))) * a_ref[slot]).astype(BF)

    def down(c, slot):
        acc_ref[...] += _dot(act_ref[slot], wd_ref[c])

    up(0, 0)
    up(1, 1)
    activate(0, 0)

    for c in range(FF_NCHUNK - 2):
        down(c, c % 2)
        activate(c + 1, (c + 1) % 2)
        up(c + 2, c % 2)
    down(FF_NCHUNK - 2, (FF_NCHUNK - 2) % 2)
    activate(FF_NCHUNK - 1, (FF_NCHUNK - 1) % 2)
    down(FF_NCHUNK - 1, (FF_NCHUNK - 1) % 2)
    y = xc_ref[...] + acc_ref[...]
    if final:
        y = _rms(y, fw_ref[...])
    o_ref[...] = y


def _ffn(x, p, final_w, tm, final):
    B, S, _ = x.shape
    per = tm // HALO
    nh = S // HALO
    consts = (p["norm_ffn_w"], p["up_a"], p["up_g"], p["conv_w"], p["conv_b"], p["down"], final_w)
    return pl.pallas_call(
        functools.partial(_ffn_kernel, tm=tm, final=final), grid=(B, S // tm),
        in_specs=[pl.BlockSpec((None, HALO, D_MODEL), lambda b, j: (b, jnp.maximum(j * per - 1, 0), 0)),
                  pl.BlockSpec((None, tm, D_MODEL), lambda b, j: (b, j, 0)),
                  pl.BlockSpec((None, HALO, D_MODEL), lambda b, j: (b, jnp.minimum((j + 1) * per, nh - 1), 0))]
        + [_const_spec(a.shape) for a in consts],
        out_specs=pl.BlockSpec((None, tm, D_MODEL), lambda b, j: (b, j, 0)),
        out_shape=jax.ShapeDtypeStruct(x.shape, F32),
        scratch_shapes=[pltpu.VMEM((tm + 2 * HALO, D_MODEL), BF), pltpu.VMEM((2, tm + 2 * HALO, FF_CHUNK), F32),
                        pltpu.VMEM((2, tm, FF_CHUNK), F32), pltpu.VMEM((tm, D_MODEL), F32),
                        pltpu.VMEM((2, tm, FF_CHUNK), BF)],
        compiler_params=_cparams(("parallel", "parallel")), name="ffn")(x, x, x, *consts)


def _rope_tables(S):
    t = jnp.arange(S)
    inv = 1.0 / (ROPE_THETA ** (jnp.arange(0, ROPE_AXIS_DIM, 2, dtype=F32) / ROPE_AXIS_DIM))
    ang = jnp.concatenate([(t // GRID_W).astype(F32)[:, None] * inv,
                           (t % GRID_W).astype(F32)[:, None] * inv], axis=-1)
    cos, sin = jnp.cos(ang), jnp.sin(ang)
    cos_h = jnp.concatenate([cos, cos], axis=-1)
    sin_h = jnp.concatenate([-sin, sin], axis=-1)
    return jnp.tile(cos_h, (1, GQA_Q_HEADS)), jnp.tile(sin_h, (1, GQA_Q_HEADS))


def _layer_params(l, w):
    row = lambda a: a.reshape(1, -1).astype(F32)
    w_in = w["w_in"][l]
    qscale = 1.0 / math.sqrt(HEAD_DIM)
    wa = jnp.concatenate([w_in[:, :256] * (qscale * LOG2E), w_in[:, 256:768]], axis=1)
    wq = w_in[:, 1024:1280].reshape(D_MODEL, GQA_Q_HEADS, HEAD_DIM)[:, np.array([0, 2, 1, 3]), :].reshape(
        D_MODEL, 256)
    blk = np.arange(256) // HEAD_DIM
    p = {
        "norm_mix_w": row(w["norm_mix_w"][l]),
        "wa": wa.astype(BF),
        "wb": w_in[:, 768:1024].astype(BF),
        "wq": wq.astype(BF),
        "wk": w_in[:, 1280:1408].astype(BF),
        "wvt": w_in[:, 1408:1536].T.astype(BF),
        "wd": w_in[:, 1536:].astype(BF),
        "ones": jnp.asarray(blk[:, None] == blk[None, :], BF),
        "gq": row(jnp.tile(w["gqa_q_norm_w"][l], GQA_Q_HEADS)),
        "gk": row(jnp.tile(w["gqa_k_norm_w"][l], GQA_Q_HEADS)),
        "na_bias": _na_bias_table(w["na_rpb"][l].astype(F32)),
        "s5_f": _s5_tables(*[w[k][l, 0] for k in ("s5_lambda_re", "s5_lambda_im", "s5_log_dt", "s5_b_re",
                                                    "s5_b_im", "s5_c_re", "s5_c_im")], rev=False),
        "s5_b": _s5_tables(*[w[k][l, 1] for k in ("s5_lambda_re", "s5_lambda_im", "s5_log_dt", "s5_b_re",
                                                    "s5_b_im", "s5_c_re", "s5_c_im")], rev=True),
        "s5_d": row(w["s5_d"][l]),
        "s5_glu_w": w["s5_glu_w"][l].astype(BF),
        "s5_glu_b": row(w["s5_glu_b"][l]),
        "w_out": w["w_out"][l].astype(BF),
        "norm_xattn_w": row(w["norm_xattn_w"][l]),
        "norm_mem_w": row(w["norm_mem_w"][l]),
        "xq": (w["xattn_w_q"][l] * qscale).astype(BF),
        "xkv": w["xattn_w_kv"][l].astype(BF),
        "xo": w["xattn_w_o"][l].astype(BF),
        "norm_ffn_w": row(w["norm_ffn_w"][l]),
        "up_a": w["ffn_w_up"][l][:, :D_FF].reshape(D_MODEL, FF_NCHUNK, FF_CHUNK).transpose(1, 0, 2).astype(BF),
        "up_g": w["ffn_w_up"][l][:, D_FF:].reshape(D_MODEL, FF_NCHUNK, FF_CHUNK).transpose(1, 0, 2).astype(BF),
        "conv_w": w["ffn_conv_w"][l].astype(F32).reshape(3, FF_NCHUNK, FF_CHUNK).transpose(1, 0, 2),
        "conv_b": w["ffn_conv_b"][l].astype(F32).reshape(FF_NCHUNK, 1, FF_CHUNK),
        "down": w["ffn_w_down"][l].reshape(FF_NCHUNK, FF_CHUNK, D_MODEL).astype(BF),
    }
    g = w["mix_out_norm_w"][l]
    p["g_a"], p["g_b"], p["g_c"], p["g_d"] = (row(g[i * MIXW:(i + 1) * MIXW]) for i in range(4))
    sm = jax.nn.softmax(w["hgrn_lower_bound"].astype(F32), axis=0)
    lb = jnp.concatenate([jnp.zeros_like(sm[:1]), jnp.cumsum(sm, axis=0)[:-1]], axis=0)[l]
    p["lb_tab"] = jnp.concatenate(
        [row(jnp.log(lb)), row(jnp.log1p(-lb)), row(1.0 - lb), jnp.zeros((SUBLANES - 3, MIXW), F32)], axis=0)
    return p


def _tiles(S):
    return {"tm": 512, "s5_l": 256, "tq": 256, "tk": 512, "hg_tb": 512}


def _encode(x, mem, layers, final_w):
    B, S, _ = x.shape
    t = _tiles(S)
    cos, sin = _rope_tables(S)
    depth = len(layers)
    for l, p in enumerate(layers):
        pp = dict(p, cos=cos, sin=sin)
        batch_rows = B % SUBLANES == 0
        qkva, ub, qc, kc, vct, dd = _in_proj(x, pp, t["tm"], batch_rows)
        ma = _na(qkva, p["na_bias"], p["g_a"])
        if batch_rows:
            ub = ub.reshape(B // SUBLANES, S * SUBLANES, MIXW)
        mb = _s5(ub, p["s5_f"], p["s5_b"], p["s5_d"], p["s5_glu_w"], p["s5_glu_b"], p["g_b"], t["s5_l"],
                 batch_rows)
        if batch_rows:
            mb = mb.reshape(B // SUBLANES, S, SUBLANES * MIXW)
        mc = _gqa(qc, kc, vct, p["g_c"], t["tq"], t["tk"])
        md = _hgrn(dd, p["lb_tab"], p["g_d"], p["ones"], t["hg_tb"])
        kv = _mem_kv(mem, p["norm_mem_w"], p["xkv"])
        x = _post(x, ma, mb, mc, md, kv, p, t["tm"])
        x = _ffn(x, p, final_w, t["tm"], final=(l == depth - 1))
    return x


def kernel(x_prompt, x_sample, mem_prompt, mem_sample, norm_mix_w, w_in, na_rpb, s5_lambda_re, s5_lambda_im, s5_log_dt, s5_b_re, s5_b_im, s5_c_re, s5_c_im, s5_d, s5_glu_w, s5_glu_b, gqa_q_norm_w, gqa_k_norm_w, hgrn_lower_bound, mix_out_norm_w, w_out, norm_xattn_w, norm_mem_w, xattn_w_q, xattn_w_kv, xattn_w_o, norm_ffn_w, ffn_w_up, ffn_conv_w, ffn_conv_b, ffn_w_down, final_norm_w):
    w = dict(norm_mix_w=norm_mix_w, w_in=w_in, na_rpb=na_rpb, s5_lambda_re=s5_lambda_re,
             s5_lambda_im=s5_lambda_im, s5_log_dt=s5_log_dt, s5_b_re=s5_b_re, s5_b_im=s5_b_im,
             s5_c_re=s5_c_re, s5_c_im=s5_c_im, s5_d=s5_d, s5_glu_w=s5_glu_w, s5_glu_b=s5_glu_b,
             gqa_q_norm_w=gqa_q_norm_w, gqa_k_norm_w=gqa_k_norm_w, hgrn_lower_bound=hgrn_lower_bound,
             mix_out_norm_w=mix_out_norm_w, w_out=w_out, norm_xattn_w=norm_xattn_w, norm_mem_w=norm_mem_w,
             xattn_w_q=xattn_w_q, xattn_w_kv=xattn_w_kv, xattn_w_o=xattn_w_o, norm_ffn_w=norm_ffn_w,
             ffn_w_up=ffn_w_up, ffn_conv_w=ffn_conv_w, ffn_conv_b=ffn_conv_b, ffn_w_down=ffn_w_down)
    layers = [_layer_params(l, w) for l in range(w_in.shape[0])]
    final_w = final_norm_w.reshape(1, -1).astype(F32)
    return (_encode(x_prompt, mem_prompt, layers, final_w), _encode(x_sample, mem_sample, layers, final_w))
```

```python
import functools
import math

import numpy as np
import jax
import jax.numpy as jnp
from jax import lax
from jax.experimental import pallas as pl
from jax.experimental.pallas import tpu as pltpu

D_MODEL = 1024
GRID_W = 64
HEAD_DIM = 64
EPS = 1e-6
NA_HEADS = 4
NA_WIN_ROWS = 8
NA_WIN_COLS = 16
S5_WIDTH = 256
S5_GROUP = 16
S5_GROUPS = 16
S5_STATE = 64
S5_NSTATE = S5_GROUPS * S5_STATE
GQA_Q_HEADS = 4
GQA_KV_HEADS = 2
ROPE_THETA = 10000.0
ROPE_AXIS_DIM = HEAD_DIM // 2
HGRN_HEADS = 4
XA_HEADS = 4
MEM_TOKENS = 256
D_FF = 2816
MIXW = 256
NEG = -1e30
LOG2E = 1.4426950408889634

SUBLANES = 8
FF_CHUNK = 256
FF_NCHUNK = D_FF // FF_CHUNK
HALO = 16
HG_CHUNK = 64
HG_LEVELS = (32, 16, 8)
VMEM_LIMIT = 56 * 1024 * 1024

BF = jnp.bfloat16
F32 = jnp.float32


def _cparams(sem):
    return pltpu.CompilerParams(dimension_semantics=sem, vmem_limit_bytes=VMEM_LIMIT)


def _dot(a, b):
    return jnp.dot(a, b, preferred_element_type=F32)


def _dot_nt(a, b):
    return lax.dot_general(a, b, (((1,), (1,)), ((), ())), preferred_element_type=F32)


def _dot_tn(a, b):
    return lax.dot_general(a, b, (((0,), (0,)), ((), ())), preferred_element_type=F32)


def _split3(x):
    h1 = x.astype(BF)
    r1 = x - h1.astype(F32)
    h2 = r1.astype(BF)
    r2 = r1 - h2.astype(F32)
    return h1, h2, r2.astype(BF)


def _rms(x, w):
    ms = jnp.mean(x * x, axis=-1, keepdims=True)
    return x * lax.rsqrt(ms + EPS) * w


def _head_mask(shape, axis, h):
    idx = lax.broadcasted_iota(jnp.int32, shape, axis)
    return (idx >= h * HEAD_DIM) & (idx < (h + 1) * HEAD_DIM)


def _const_spec(shape):
    nd = len(shape)
    return pl.BlockSpec(shape, lambda *_: (0,) * nd)


def _in_proj_kernel(x_ref, nw_ref, wa_ref, wb_ref, wq_ref, wk_ref, wvt_ref, wd_ref, ones_ref,
                    gq_ref, gk_ref, cos_ref, sin_ref,
                    qkva_ref, ub_ref, qc_ref, kc_ref, vct_ref, dd_ref):
    x = x_ref[...]
    h = _rms(x, nw_ref[...]).astype(BF)
    ones = ones_ref[...]
    cos = cos_ref[...]
    sin = sin_ref[...]

    def norm_rope(y, g, width):
        yy = y * y
        hi = yy.astype(BF)
        lo = (yy - hi.astype(F32)).astype(BF)
        ms = (_dot(hi, ones[:width, :width]) + _dot(lo, ones[:width, :width])) * (1.0 / HEAD_DIM)
        yn = y * lax.rsqrt(ms + EPS) * g
        lane = lax.broadcasted_iota(jnp.int32, yn.shape, 1)
        first_half = (lane % HEAD_DIM) < ROPE_AXIS_DIM
        swapped = jnp.where(first_half,
                            pltpu.roll(yn, width - ROPE_AXIS_DIM, axis=1),
                            pltpu.roll(yn, ROPE_AXIS_DIM, axis=1))
        return yn * cos[:, :width] + swapped * sin[:, :width]

    q = norm_rope(_dot(h, wq_ref[...]), gq_ref[...], 4 * HEAD_DIM)
    qc_ref[...] = (q * (LOG2E / math.sqrt(HEAD_DIM))).astype(BF)
    k = norm_rope(_dot(h, wk_ref[...]), gk_ref[...][:, :2 * HEAD_DIM], 2 * HEAD_DIM)
    kc_ref[...] = k.astype(BF)
    qkva_ref[...] = _dot(h, wa_ref[...]).astype(BF)
    ub_ref[...] = _dot(h, wb_ref[...])
    dd_ref[...] = _dot(h, wd_ref[...])
    vct_ref[...] = _dot_nt(wvt_ref[...], h).astype(BF)


def _batch_rows_spec(tm):
    return pl.BlockSpec((None, tm, MIXW), lambda b, j: (b // SUBLANES, j, b % SUBLANES))


def _in_proj(x, p, tm, ub_batch_rows):
    B, S, _ = x.shape
    grid = (B, S // tm)
    tok = lambda w: pl.BlockSpec((None, tm, w), lambda b, j: (b, j, 0))
    ub_shape = (B // SUBLANES, S, SUBLANES * MIXW) if ub_batch_rows else (B, S, MIXW)
    ub_spec = _batch_rows_spec(tm) if ub_batch_rows else tok(MIXW)
    tab = pl.BlockSpec((tm, 4 * HEAD_DIM), lambda b, j: (j, 0))
    ins = [x, p["norm_mix_w"], p["wa"], p["wb"], p["wq"], p["wk"], p["wvt"], p["wd"], p["ones"],
           p["gq"], p["gk"], p["cos"], p["sin"]]
    in_specs = [tok(D_MODEL)] + [_const_spec(a.shape) for a in ins[1:11]] + [tab, tab]
    out_shape = (
        jax.ShapeDtypeStruct((B, S, 768), BF),
        jax.ShapeDtypeStruct(ub_shape, F32),
        jax.ShapeDtypeStruct((B, S, 256), BF),
        jax.ShapeDtypeStruct((B, S, 128), BF),
        jax.ShapeDtypeStruct((B, 128, S), BF),
        jax.ShapeDtypeStruct((B, S, 1280), F32),
    )
    out_specs = (tok(768), ub_spec, tok(256), tok(128),
                 pl.BlockSpec((None, 128, tm), lambda b, j: (b, 0, j)), tok(1280))
    return pl.pallas_call(
        _in_proj_kernel, grid=grid, in_specs=in_specs, out_specs=out_specs, out_shape=out_shape,
        compiler_params=_cparams(("parallel", "parallel")), name="in_proj")(*ins)


NA_CHUNK_ROWS = 8
NA_CHUNK = NA_CHUNK_ROWS * GRID_W
NA_KEYS = NA_WIN_ROWS * GRID_W


def _na_kernel(q_ref, kp_ref, kc_ref, kn_ref, vp_ref, vc_ref, vn_ref, bias_ref, g_ref, o_ref,
               kbuf, vbuf, *, rows):
    j = pl.program_id(1)
    kbuf[0:NA_CHUNK, :] = kp_ref[...]
    kbuf[NA_CHUNK:2 * NA_CHUNK, :] = kc_ref[...]
    kbuf[2 * NA_CHUNK:3 * NA_CHUNK, :] = kn_ref[...]
    for part, ref in enumerate((vp_ref, vc_ref, vn_ref)):
        v = ref[...]
        for h in range(NA_HEADS):
            vbuf[h, part * NA_CHUNK:(part + 1) * NA_CHUNK, :] = jnp.where(
                _head_mask(v.shape, 1, h), v, jnp.zeros_like(v))
    gain = g_ref[...]

    def row_body(rr, carry):
        r = j * NA_CHUNK_ROWS + rr
        r0 = jnp.clip(r - NA_WIN_ROWS // 2, 0, rows - NA_WIN_ROWS)
        off = pl.multiple_of((r0 - j * NA_CHUNK_ROWS + NA_CHUNK_ROWS) * GRID_W, GRID_W)
        variant = r - r0
        qrow = pl.multiple_of(rr * GRID_W, GRID_W)
        q = q_ref[pl.ds(qrow, GRID_W), :]
        kw = kbuf[pl.ds(off, NA_KEYS), :]
        qstack = jnp.concatenate(
            [jnp.where(_head_mask(q.shape, 1, h), q, jnp.zeros_like(q)) for h in range(NA_HEADS)], axis=0)
        s = _dot_nt(qstack, kw) + bias_ref[variant]
        m = jnp.max(s, axis=-1, keepdims=True)
        e = jnp.exp2(s - m)
        p = (e * (1.0 / jnp.sum(e, axis=-1, keepdims=True))).astype(BF)
        acc = _dot(p[0:GRID_W], vbuf[0, pl.ds(off, NA_KEYS), :])
        for h in range(1, NA_HEADS):
            acc = acc + _dot(p[h * GRID_W:(h + 1) * GRID_W], vbuf[h, pl.ds(off, NA_KEYS), :])
        o_ref[pl.ds(qrow, GRID_W), :] = _rms(acc, gain).astype(BF)
        return carry

    lax.fori_loop(0, NA_CHUNK_ROWS, row_body, 0, unroll=True)


def _na_bias_table(rpb):
    qcols = np.arange(GRID_W)
    kcols = np.arange(GRID_W)
    c0 = np.clip(qcols - NA_WIN_COLS // 2, 0, GRID_W - NA_WIN_COLS)
    in_win = (kcols[None, :] >= c0[:, None]) & (kcols[None, :] < c0[:, None] + NA_WIN_COLS)
    dc = np.clip(kcols[None, :] - qcols[:, None], -(NA_WIN_COLS - 1), NA_WIN_COLS - 1) + (NA_WIN_COLS - 1)
    onehot = jnp.asarray(dc[None, :, :] == np.arange(2 * NA_WIN_COLS - 1)[:, None, None], F32)
    cols = jnp.einsum('hrc,cqk->hrqk', rpb, onehot, precision=lax.Precision.HIGHEST) * LOG2E
    cols = jnp.where(in_win[None, None], cols, NEG)
    variants = []
    for var in range(NA_WIN_ROWS):
        lo = NA_WIN_ROWS - 1 - var
        bias = cols[:, lo:lo + NA_WIN_ROWS]
        variants.append(bias.transpose(0, 2, 1, 3).reshape(NA_HEADS * GRID_W, NA_KEYS))
    return jnp.stack(variants)


def _na(qkva, bias, gain):
    B, S, _ = qkva.shape
    rows = S // GRID_W
    nchunk = S // NA_CHUNK
    blk = lambda col, f: pl.BlockSpec((None, NA_CHUNK, MIXW), lambda b, j: (b, f(j), col))
    prev = lambda j: jnp.maximum(j - 1, 0)
    cur = lambda j: j
    nxt = lambda j: jnp.minimum(j + 1, nchunk - 1)
    in_specs = [blk(0, cur), blk(1, prev), blk(1, cur), blk(1, nxt), blk(2, prev), blk(2, cur), blk(2, nxt),
                _const_spec(bias.shape), _const_spec(gain.shape)]
    return pl.pallas_call(
        functools.partial(_na_kernel, rows=rows), grid=(B, nchunk), in_specs=in_specs,
        out_specs=pl.BlockSpec((None, NA_CHUNK, MIXW), lambda b, j: (b, j, 0)),
        out_shape=jax.ShapeDtypeStruct((B, S, MIXW), BF),
        scratch_shapes=[pltpu.VMEM((3 * NA_CHUNK, MIXW), BF), pltpu.VMEM((NA_HEADS, 3 * NA_CHUNK, MIXW), BF)],
        compiler_params=_cparams(("parallel", "parallel")), name="na")(
            qkva, qkva, qkva, qkva, qkva, qkva, qkva, bias, gain)


S5_LANES = 256


def _s5_kernel(*refs, rev, ntile, batch_rows):
    if rev:
        (u_ref, yf_ref, bbd_ref, cc_ref, tab_ref, d_ref, gw_ref, gb_ref, g_ref, o_ref, z_ref, carry_ref) = refs
    else:
        (u_ref, bbd_ref, cc_ref, tab_ref, o_ref, z_ref, carry_ref) = refs
    j = pl.program_id(1)

    @pl.when(j == 0)
    def _():
        carry_ref[...] = jnp.zeros_like(carry_ref)

    u = u_ref[...]
    ub = u.astype(BF)
    y = None
    for c0 in range(0, S5_NSTATE, S5_LANES):
        re = slice(c0, c0 + S5_LANES)
        im = slice(S5_NSTATE + c0, S5_NSTATE + c0 + S5_LANES)
        z_ref[:, re] = _dot(ub, bbd_ref[:, re])
        z_ref[:, im] = _dot(ub, bbd_ref[:, im])
        steps = [(tab_ref[2 * i, :, re], tab_ref[2 * i + 1, :, re], 1 << i) for i in range(3)]
        pr, pi = tab_ref[6, :, re], tab_ref[7, :, re]
        cr, ci = carry_ref[:, re], carry_ref[:, im]
        last = 0 if rev else SUBLANES - 1
        if batch_rows:
            first = SUBLANES - 1 - last
            pr = jnp.broadcast_to(pr[first:first + 1, :], pr.shape)
            pi = jnp.broadcast_to(pi[first:first + 1, :], pi.shape)
        for i in range(ntile):
            t = (ntile - 1 - i) if rev else i
            rows = slice(t * SUBLANES, (t + 1) * SUBLANES)
            zr = z_ref[rows, re]
            zi = z_ref[rows, im]
            if not batch_rows:
                for ar, ai, sh in steps:
                    amt = SUBLANES - sh if rev else sh
                    sr = pltpu.roll(zr, amt, axis=0)
                    si = pltpu.roll(zi, amt, axis=0)
                    zr, zi = zr + ar * sr - ai * si, zi + ar * si + ai * sr
            zr, zi = zr + pr * cr - pi * ci, zi + pr * ci + pi * cr
            z_ref[rows, re] = zr
            z_ref[rows, im] = zi
            if batch_rows:
                cr, ci = zr, zi
            else:
                cr = jnp.broadcast_to(zr[last:last + 1, :], zr.shape)
                ci = jnp.broadcast_to(zi[last:last + 1, :], zi.shape)
        carry_ref[:, re] = cr
        carry_ref[:, im] = ci
        part = _dot(z_ref[:, re].astype(BF), cc_ref[re, :]) + _dot(z_ref[:, im].astype(BF), cc_ref[im, :])
        y = part if y is None else y + part
    if not rev:
        o_ref[...] = y
    else:
        y = y + yf_ref[...] + d_ref[...] * u
        hcur = 0.5 * y * (1.0 + jnp.tanh(math.sqrt(2.0 / math.pi) * (y + 0.044715 * (y * y * y))))
        gate = _dot(hcur.astype(BF), gw_ref[...]) + gb_ref[...]
        out = hcur * (1.0 / (1.0 + jnp.exp(-gate)))
        o_ref[...] = _rms(out, g_ref[...]).astype(BF)


def _s5_tables(lam_re, lam_im, log_dt, b_re, b_im, c_re, c_im, rev):
    lre = jnp.minimum(lam_re.astype(F32), -1e-4)
    lim = lam_im.astype(F32)
    dt = jnp.exp(log_dt.astype(F32))[:, None]
    mag = jnp.exp(lre * dt)
    ar = mag * jnp.cos(lim * dt)
    ai = mag * jnp.sin(lim * dt)
    den = lre * lre + lim * lim
    nre = ar - 1.0
    coef_re = (nre * lre + ai * lim) / den
    coef_im = (ai * lre - nre * lim) / den
    bre, bim = b_re.astype(F32), b_im.astype(F32)
    bbar_re = coef_re[..., None] * bre - coef_im[..., None] * bim
    bbar_im = coef_re[..., None] * bim + coef_im[..., None] * bre
    eye = jnp.eye(S5_GROUPS, dtype=F32)
    bd = lambda m: jnp.einsum('gnc,gh->gchn', m, eye).reshape(S5_WIDTH, S5_NSTATE)
    bbd = jnp.concatenate([bd(bbar_re), bd(bbar_im)], axis=1).astype(BF)
    cd = lambda m: jnp.einsum('gcn,gh->gnhc', m.astype(F32), eye).reshape(S5_NSTATE, S5_WIDTH)
    cc = jnp.concatenate([cd(c_re), -cd(c_im)], axis=0).astype(BF)
    a1r, a1i = ar.reshape(1, -1), ai.reshape(1, -1)
    cmul = lambda xr, xi, yr, yi: (xr * yr - xi * yi, xr * yi + xi * yr)
    a2r, a2i = cmul(a1r, a1i, a1r, a1i)
    a4r, a4i = cmul(a2r, a2i, a2r, a2i)
    pw = [(a1r, a1i)]
    for _ in range(SUBLANES - 1):
        pw.append(cmul(pw[-1][0], pw[-1][1], a1r, a1i))
    pr = jnp.concatenate([x[0] for x in pw], axis=0)
    pi = jnp.concatenate([x[1] for x in pw], axis=0)
    rows = np.arange(SUBLANES)[:, None]
    tabs = []
    for (xr, xi), sh in (((a1r, a1i), 1), ((a2r, a2i), 2), ((a4r, a4i), 4)):
        keep = jnp.asarray(rows >= sh, F32)
        tabs += [keep * xr, keep * xi]
    tabs += [pr, pi]
    tab = jnp.stack(tabs)
    if rev:
        tab = tab[:, ::-1, :]
    return bbd, cc, tab


def _s5(u, tabs_f, tabs_b, d_skip, glu_w, glu_b, gain, L, batch_rows):
    G, R, _ = u.shape
    nch = R // L
    ntile = L // SUBLANES
    scratch = [pltpu.VMEM((L, 2 * S5_NSTATE), F32), pltpu.VMEM((SUBLANES, 2 * S5_NSTATE), F32)]
    fwd_blk = pl.BlockSpec((None, L, S5_WIDTH), lambda b, j: (b, j, 0))
    bwd_blk = pl.BlockSpec((None, L, S5_WIDTH), lambda b, j: (b, nch - 1 - j, 0))
    kern = functools.partial(_s5_kernel, ntile=ntile, batch_rows=batch_rows)
    yf = pl.pallas_call(
        functools.partial(kern, rev=False), grid=(G, nch),
        in_specs=[fwd_blk] + [_const_spec(a.shape) for a in tabs_f],
        out_specs=fwd_blk, out_shape=jax.ShapeDtypeStruct((G, R, S5_WIDTH), F32),
        scratch_shapes=scratch, compiler_params=_cparams(("parallel", "arbitrary")), name="s5_fwd")(u, *tabs_f)
    rest = (d_skip, glu_w, glu_b, gain)
    return pl.pallas_call(
        functools.partial(kern, rev=True), grid=(G, nch),
        in_specs=[bwd_blk, bwd_blk] + [_const_spec(a.shape) for a in tabs_b + rest],
        out_specs=bwd_blk, out_shape=jax.ShapeDtypeStruct((G, R, S5_WIDTH), BF),
        scratch_shapes=scratch, compiler_params=_cparams(("parallel", "arbitrary")), name="s5_bwd")(
            u, yf, *tabs_b, *rest)


GQA_ACC_ROWS = HEAD_DIM + 16


def _gqa_kernel(q_ref, k_ref, vt_ref, g_ref, o_ref, s_ref, p_ref, m_ref, a_ref, a2_ref, acc_ref, *, tk, nkv):
    q = q_ref[...]
    tq = q.shape[0]
    halves = (q[:, :128], q[:, 128:])
    left = lax.broadcasted_iota(jnp.int32, (tq, 128), 1) < HEAD_DIM
    zero = jnp.zeros((tq, 128), BF)
    qs = [jnp.where(left, halves[0], zero), jnp.where(left, zero, halves[0]),
          jnp.where(left, halves[1], zero), jnp.where(left, zero, halves[1])]
    m_ref[...] = jnp.full(m_ref.shape, -jnp.inf, F32)
    acc_ref[...] = jnp.zeros(acc_ref.shape, F32)
    ones_rows = jnp.ones((GQA_ACC_ROWS - HEAD_DIM, tk), BF)

    def scores(jb, slot):
        k = k_ref[pl.ds(pl.multiple_of(jb * tk, tk), tk), :]
        for i in range(4):
            st = _dot_nt(k, qs[i])
            s_ref[slot, i] = st
            m_old = m_ref[i:i + 1, :]
            m_new = jnp.maximum(m_old, jnp.max(st, axis=0, keepdims=True))
            a_ref[i:i + 1, :] = jnp.exp2(m_old - m_new)
            m_ref[i:i + 1, :] = m_new

    def probs(slot):
        for i in range(4):
            p_ref[slot, i] = jnp.exp2(s_ref[slot, i] - m_ref[i:i + 1, :]).astype(BF)
            a2_ref[i:i + 1, :] = a_ref[i:i + 1, :]

    def values(jb, slot):
        vt = vt_ref[:, pl.ds(pl.multiple_of(jb * tk, tk), tk)]
        lhs = [jnp.concatenate([vt[g * HEAD_DIM:(g + 1) * HEAD_DIM, :], ones_rows], axis=0)
               for g in range(GQA_KV_HEADS)]
        for i in range(4):
            pv = _dot(lhs[i % 2], p_ref[slot, i])
            acc_ref[i] = a2_ref[i:i + 1, :] * acc_ref[i] + pv

    scores(0, 0)
    probs(0)
    scores(1, 1)

    def stage_triple(jb, slot):
        values(jb, slot)
        probs(1 - slot)
        scores(jb + 2, slot)

    def kv_body(jj, carry):
        stage_triple(2 * jj, 0)
        stage_triple(2 * jj + 1, 1)
        return carry

    lax.fori_loop(0, (nkv - 2) // 2, kv_body, 0)
    values(nkv - 2, (nkv - 2) % 2)
    probs((nkv - 1) % 2)
    values(nkv - 1, (nkv - 1) % 2)
    outs = [acc_ref[i, 0:HEAD_DIM, :] * (1.0 / acc_ref[i, HEAD_DIM:HEAD_DIM + 1, :]) for i in range(4)]
    ot = jnp.concatenate([outs[0], outs[2], outs[1], outs[3]], axis=0)
    o_ref[...] = _rms(ot.T, g_ref[...]).astype(BF)


def _gqa(qc, kc, vct, gain, tq, tk):
    B, S, _ = qc.shape
    assert S % (2 * tk) == 0 and S // tk >= 2, "the kv loop handles two blocks per trip"
    return pl.pallas_call(
        functools.partial(_gqa_kernel, tk=tk, nkv=S // tk), grid=(B, S // tq),
        in_specs=[pl.BlockSpec((None, tq, 256), lambda b, j: (b, j, 0)),
                  pl.BlockSpec((None, S, 128), lambda b, j: (b, 0, 0)),
                  pl.BlockSpec((None, 128, S), lambda b, j: (b, 0, 0)),
                  _const_spec(gain.shape)],
        out_specs=pl.BlockSpec((None, tq, MIXW), lambda b, j: (b, j, 0)),
        out_shape=jax.ShapeDtypeStruct((B, S, MIXW), BF),
        scratch_shapes=[pltpu.VMEM((2, 4, tk, tq), F32), pltpu.VMEM((2, 4, tk, tq), BF),
                        pltpu.VMEM((SUBLANES, tq), F32), pltpu.VMEM((SUBLANES, tq), F32),
                        pltpu.VMEM((SUBLANES, tq), F32), pltpu.VMEM((4, GQA_ACC_ROWS, tq), F32)],
        compiler_params=_cparams(("parallel", "parallel")), name="gqa")(qc, kc, vct, gain)


def _hgrn_level_masks(rev):
    t = np.arange(HG_CHUNK)[:, None]
    s = np.arange(HG_CHUNK)[None, :]
    out = []
    for m in HG_LEVELS:
        bt, bs = t // m, s // m
        ok = ((bt % 2 == 0) & (bs == bt + 1)) if rev else ((bt % 2 == 1) & (bs == bt - 1))
        out.append(np.tile(ok.astype(np.float32), (1, HGRN_HEADS)))
    for dlt in range(SUBLANES):
        ok = (s == (t + dlt if rev else t - dlt)) & (s // SUBLANES == t // SUBLANES)
        out.append(np.tile(ok.astype(np.float32), (1, HGRN_HEADS)))
    return np.stack(out)


def _hgrn_kernel(*refs, rev, nchunk):
    if rev:
        (q_ref, z_ref, v_ref, gate_ref, of_ref, lb_ref, tri_ref, ones_ref, lmask_ref, g_ref,
         o_ref, st_ref) = refs
    else:
        (q_ref, z_ref, v_ref, lb_ref, tri_ref, ones_ref, lmask_ref, o_ref, st_ref) = refs
    C = HG_CHUNK
    j = pl.program_id(1)

    @pl.when(j == 0)
    def _():
        st_ref[...] = jnp.zeros_like(st_ref)

    log_lb = lb_ref[0:1, :]
    log1m_lb = lb_ref[1:2, :]
    one_m_lb = lb_ref[2:3, :]
    tri = tri_ref[...]
    ones = ones_ref[...]
    row_in_tile = lax.broadcasted_iota(jnp.int32, (C, MIXW), 0) % SUBLANES
    bd_mask = (lax.broadcasted_iota(jnp.int32, (MIXW, MIXW), 0) // HEAD_DIM
               == lax.broadcasted_iota(jnp.int32, (MIXW, MIXW), 1) // HEAD_DIM)
    zero_row = jnp.zeros((1, MIXW), F32)

    def tile_roll(x, amt):
        return jnp.concatenate(
            [pltpu.roll(x[r:r + SUBLANES, :], amt, axis=0) for r in range(0, C, SUBLANES)], axis=0)

    def stack_heads(x):
        xb = x.astype(BF)
        return jnp.concatenate(
            [jnp.where(_head_mask(xb.shape, 1, h), xb, jnp.zeros_like(xb)) for h in range(HGRN_HEADS)], axis=0)

    def chunk_body(ci, carry):
        c = (nchunk - 1 - ci) if rev else ci
        rows = pl.ds(pl.multiple_of(c * C, C), C)
        q = q_ref[rows, :]
        z = z_ref[rows, :]
        v = v_ref[rows, :]
        ls = jnp.minimum(z, 0.0) - jnp.log(1.0 + jnp.exp(-jnp.abs(z)))
        bv = log1m_lb + ls
        logf = jnp.maximum(log_lb, bv) + jnp.log(1.0 + jnp.exp(-jnp.abs(log_lb - bv)))
        kin = one_m_lb * jnp.exp(ls - z)
        pieces = _dot(tri, jnp.concatenate(_split3(logf * LOG2E), axis=1))
        b = pieces[:, 0:MIXW] + pieces[:, MIXW:2 * MIXW] + pieces[:, 2 * MIXW:3 * MIXW]
        edge = b[0:1, :] if rev else b[C - 1:C, :]

        st = st_ref[...]
        o = _dot_nt((q * jnp.exp2(b)).astype(BF), st.astype(BF))

        att = jnp.zeros((C, HGRN_HEADS * C), F32)
        for li, m in enumerate(HG_LEVELS):
            nb = C // m
            qrefs, krefs = [], []
            for blk in range(nb):
                if rev:
                    qrow = (blk + 1) * m
                    krow = blk * m
                else:
                    qrow = blk * m - 1
                    krow = (blk + 1) * m - 1
                qr = zero_row if (qrow < 0 or qrow >= C) else b[qrow:qrow + 1, :]
                qrefs.append(jnp.broadcast_to(qr, (m, MIXW)))
                krefs.append(jnp.broadcast_to(b[krow:krow + 1, :], (m, MIXW)))
            bq = jnp.concatenate(qrefs, axis=0)
            bk = jnp.concatenate(krefs, axis=0)
            ql = (q * jnp.exp2(b - bq)).astype(BF)
            kl = kin * jnp.exp2(bk - b)
            att = att + lmask_ref[li] * _dot_nt(ql, stack_heads(kl))
        ds = [(q * kin).astype(BF)]
        for dlt in range(1, SUBLANES):
            amt = SUBLANES - dlt if rev else dlt
            valid = (row_in_tile + dlt <= SUBLANES - 1) if rev else (row_in_tile >= dlt)
            ks = tile_roll(kin, amt)
            bs = tile_roll(b, amt)
            ds.append((q * ks * jnp.exp2(jnp.where(valid, b - bs, 0.0))).astype(BF))
        hsum = _dot(jnp.concatenate(ds, axis=0), ones)
        for dlt in range(SUBLANES):
            att = att + lmask_ref[len(HG_LEVELS) + dlt] * hsum[dlt * C:(dlt + 1) * C, :]
        o = o + _dot(att.astype(BF), stack_heads(v))

        kb = (kin * jnp.exp2(edge - b)).astype(BF)
        upd = _dot_tn(v.astype(BF), kb)
        st_ref[...] = st * jnp.exp2(edge) + jnp.where(bd_mask, upd, 0.0)

        if rev:
            tot = o + of_ref[rows, :]
            gate = gate_ref[rows, :]
            silu = gate * (1.0 / (1.0 + jnp.exp(-gate)))
            o_ref[rows, :] = (_rms(tot, g_ref[...]) * silu).astype(BF)
        else:
            o_ref[rows, :] = o
        return carry

    lax.fori_loop(0, nchunk, chunk_body, 0, unroll=True)


def _hgrn(dd, lb_tab, gain, ones, tb):
    B, S, _ = dd.shape
    nblk = S // tb
    nchunk = tb // HG_CHUNK
    tri_f = jnp.asarray(np.tril(np.ones((HG_CHUNK, HG_CHUNK), np.float32)), BF)
    tri_b = jnp.asarray(np.triu(np.ones((HG_CHUNK, HG_CHUNK), np.float32)), BF)
    lm_f = jnp.asarray(_hgrn_level_masks(False))
    lm_b = jnp.asarray(_hgrn_level_masks(True))
    fcol = lambda col: pl.BlockSpec((None, tb, MIXW), lambda b, j: (b, j, col))
    bcol = lambda col: pl.BlockSpec((None, tb, MIXW), lambda b, j: (b, nblk - 1 - j, col))
    scratch = [pltpu.VMEM((MIXW, MIXW), F32)]
    consts_f = (lb_tab, tri_f, ones, lm_f)
    of = pl.pallas_call(
        functools.partial(_hgrn_kernel, rev=False, nchunk=nchunk), grid=(B, nblk),
        in_specs=[fcol(0), fcol(1), fcol(3)] + [_const_spec(a.shape) for a in consts_f],
        out_specs=fcol(0), out_shape=jax.ShapeDtypeStruct((B, S, MIXW), F32),
        scratch_shapes=scratch, compiler_params=_cparams(("parallel", "arbitrary")), name="hgrn_fwd")(
            dd, dd, dd, *consts_f)
    consts_b = (lb_tab, tri_b, ones, lm_b, gain)
    return pl.pallas_call(
        functools.partial(_hgrn_kernel, rev=True, nchunk=nchunk), grid=(B, nblk),
        in_specs=[bcol(0), bcol(2), bcol(3), bcol(4), bcol(0)] + [_const_spec(a.shape) for a in consts_b],
        out_specs=bcol(0), out_shape=jax.ShapeDtypeStruct((B, S, MIXW), BF),
        scratch_shapes=scratch, compiler_params=_cparams(("parallel", "arbitrary")), name="hgrn_bwd")(
            dd, dd, dd, dd, of, *consts_b)


def _mem_kv_kernel(mem_ref, nw_ref, wkv_ref, o_ref):
    o_ref[...] = _dot(_rms(mem_ref[...], nw_ref[...]).astype(BF), wkv_ref[...]).astype(BF)


def _mem_kv(mem, nw, wkv):
    B, M, _ = mem.shape
    return pl.pallas_call(
        _mem_kv_kernel, grid=(B,),
        in_specs=[pl.BlockSpec((None, M, D_MODEL), lambda b: (b, 0, 0)), _const_spec(nw.shape),
                  _const_spec(wkv.shape)],
        out_specs=pl.BlockSpec((None, M, 2 * MIXW), lambda b: (b, 0, 0)),
        out_shape=jax.ShapeDtypeStruct((B, M, 2 * MIXW), BF),
        compiler_params=_cparams(("parallel",)), name="mem_kv")(mem, nw, wkv)


def _post_kernel(x_ref, ma_ref, mb_ref, mc_ref, md_ref, wo_ref, nx_ref, wq_ref, kv_ref, wxo_ref, o_ref):
    x1 = x_ref[...]
    for i, m_ref in enumerate((ma_ref, mb_ref, mc_ref, md_ref)):
        x1 = x1 + _dot(m_ref[...], wo_ref[i * MIXW:(i + 1) * MIXW, :])
    hq = _rms(x1, nx_ref[...]).astype(BF)
    q = _dot(hq, wq_ref[...]).astype(BF)
    kmem = kv_ref[:, :MIXW]
    vmem = kv_ref[:, MIXW:]
    acc = jnp.zeros(q.shape, F32)
    for h in range(XA_HEADS):
        qm = jnp.where(_head_mask(q.shape, 1, h), q, jnp.zeros_like(q))
        s = _dot_nt(qm, kmem)
        m = jnp.max(s, axis=-1, keepdims=True)
        e = jnp.exp(s - m)
        p = (e * (1.0 / jnp.sum(e, axis=-1, keepdims=True))).astype(BF)
        vm = jnp.where(_head_mask(vmem.shape, 1, h), vmem, jnp.zeros_like(vmem))
        acc = acc + _dot(p, vm)
    o_ref[...] = x1 + _dot(acc.astype(BF), wxo_ref[...])


def _post(x, ma, mb, mc, md, kv, p, tm):
    B, S, _ = x.shape
    tok = lambda w: pl.BlockSpec((None, tm, w), lambda b, j: (b, j, 0))
    consts = (p["w_out"], p["norm_xattn_w"], p["xq"])
    mb_spec = tok(MIXW) if mb.shape == ma.shape else _batch_rows_spec(tm)
    return pl.pallas_call(
        _post_kernel, grid=(B, S // tm),
        in_specs=[tok(D_MODEL), tok(MIXW), mb_spec, tok(MIXW), tok(MIXW)]
        + [_const_spec(a.shape) for a in consts]
        + [pl.BlockSpec((None, MEM_TOKENS, 2 * MIXW), lambda b, j: (b, 0, 0)), _const_spec(p["xo"].shape)],
        out_specs=tok(D_MODEL), out_shape=jax.ShapeDtypeStruct(x.shape, F32),
        compiler_params=_cparams(("parallel", "parallel")), name="post")(
            x, ma, mb, mc, md, *consts, kv, p["xo"])


def _ffn_kernel(xp_ref, xc_ref, xn_ref, nw_ref, wa_ref, wg_ref, cw_ref, cb_ref, wd_ref, fw_ref, o_ref,
                hn_ref, gx_ref, a_ref, act_ref, *, tm, final):
    j = pl.program_id(1)
    nj = pl.num_programs(1)
    nw = nw_ref[...]
    hp = _rms(xp_ref[...], nw)
    hnx = _rms(xn_ref[...], nw)
    hn_ref[0:HALO, :] = jnp.where(j > 0, hp, 0.0).astype(BF)
    hn_ref[HALO:HALO + tm, :] = _rms(xc_ref[...], nw).astype(BF)
    hn_ref[HALO + tm:2 * HALO + tm, :] = jnp.where(j < nj - 1, hnx, 0.0).astype(BF)

    def up(c, slot):
        a_ref[slot] = _dot(hn_ref[HALO:HALO + tm, :], wa_ref[c])
        gx_ref[slot] = _dot(hn_ref[...], wg_ref[c])

    def activate(c, slot):
        cw = cw_ref[c]
        gc = (cb_ref[c] + gx_ref[slot, HALO - 1:HALO - 1 + tm, :] * cw[0:1, :]
              + gx_ref[slot, HALO:HALO + tm, :] * cw[1:2, :]
              + gx_ref[slot, HALO + 1:HALO + 1 + tm, :] * cw[2:3, :])
        act_ref[:, c * FF_CHUNK:(c + 1) * FF_CHUNK] = (
            gc * (1.0 / (1.0 + jnp.exp(-gc))) * a_ref[slot]).astype(BF)

    up(0, 0)
    for c in range(1, FF_NCHUNK):
        up(c, c % 2)
        activate(c - 1, (c - 1) % 2)
    activate(FF_NCHUNK - 1, (FF_NCHUNK - 1) % 2)
    y = xc_ref[...] + _dot(act_ref[...], wd_ref[...])
    if final:
        y = _rms(y, fw_ref[...])
    o_ref[...] = y


def _ffn(x, p, final_w, tm, final):
    B, S, _ = x.shape
    per = tm // HALO
    nh = S // HALO
    consts = (p["norm_ffn_w"], p["up_a"], p["up_g"], p["conv_w"], p["conv_b"], p["down"], final_w)
    return pl.pallas_call(
        functools.partial(_ffn_kernel, tm=tm, final=final), grid=(B, S // tm),
        in_specs=[pl.BlockSpec((None, HALO, D_MODEL), lambda b, j: (b, jnp.maximum(j * per - 1, 0), 0)),
                  pl.BlockSpec((None, tm, D_MODEL), lambda b, j: (b, j, 0)),
                  pl.BlockSpec((None, HALO, D_MODEL), lambda b, j: (b, jnp.minimum((j + 1) * per, nh - 1), 0))]
        + [_const_spec(a.shape) for a in consts],
        out_specs=pl.BlockSpec((None, tm, D_MODEL), lambda b, j: (b, j, 0)),
        out_shape=jax.ShapeDtypeStruct(x.shape, F32),
        scratch_shapes=[pltpu.VMEM((tm + 2 * HALO, D_MODEL), BF), pltpu.VMEM((2, tm + 2 * HALO, FF_CHUNK), F32),
                        pltpu.VMEM((2, tm, FF_CHUNK), F32), pltpu.VMEM((tm, D_FF), BF)],
        compiler_params=_cparams(("parallel", "parallel")), name="ffn")(x, x, x, *consts)


def _rope_tables(S):
    t = jnp.arange(S)
    inv = 1.0 / (ROPE_THETA ** (jnp.arange(0, ROPE_AXIS_DIM, 2, dtype=F32) / ROPE_AXIS_DIM))
    ang = jnp.concatenate([(t // GRID_W).astype(F32)[:, None] * inv,
                           (t % GRID_W).astype(F32)[:, None] * inv], axis=-1)
    cos, sin = jnp.cos(ang), jnp.sin(ang)
    cos_h = jnp.concatenate([cos, cos], axis=-1)
    sin_h = jnp.concatenate([-sin, sin], axis=-1)
    return jnp.tile(cos_h, (1, GQA_Q_HEADS)), jnp.tile(sin_h, (1, GQA_Q_HEADS))


def _layer_params(l, w):
    row = lambda a: a.reshape(1, -1).astype(F32)
    w_in = w["w_in"][l]
    qscale = 1.0 / math.sqrt(HEAD_DIM)
    wa = jnp.concatenate([w_in[:, :256] * (qscale * LOG2E), w_in[:, 256:768]], axis=1)
    wq = w_in[:, 1024:1280].reshape(D_MODEL, GQA_Q_HEADS, HEAD_DIM)[:, np.array([0, 2, 1, 3]), :].reshape(
        D_MODEL, 256)
    blk = np.arange(256) // HEAD_DIM
    p = {
        "norm_mix_w": row(w["norm_mix_w"][l]),
        "wa": wa.astype(BF),
        "wb": w_in[:, 768:1024].astype(BF),
        "wq": wq.astype(BF),
        "wk": w_in[:, 1280:1408].astype(BF),
        "wvt": w_in[:, 1408:1536].T.astype(BF),
        "wd": w_in[:, 1536:].astype(BF),
        "ones": jnp.asarray(blk[:, None] == blk[None, :], BF),
        "gq": row(jnp.tile(w["gqa_q_norm_w"][l], GQA_Q_HEADS)),
        "gk": row(jnp.tile(w["gqa_k_norm_w"][l], GQA_Q_HEADS)),
        "na_bias": _na_bias_table(w["na_rpb"][l].astype(F32)),
        "s5_f": _s5_tables(*[w[k][l, 0] for k in ("s5_lambda_re", "s5_lambda_im", "s5_log_dt", "s5_b_re",
                                                    "s5_b_im", "s5_c_re", "s5_c_im")], rev=False),
        "s5_b": _s5_tables(*[w[k][l, 1] for k in ("s5_lambda_re", "s5_lambda_im", "s5_log_dt", "s5_b_re",
                                                    "s5_b_im", "s5_c_re", "s5_c_im")], rev=True),
        "s5_d": row(w["s5_d"][l]),
        "s5_glu_w": w["s5_glu_w"][l].astype(BF),
        "s5_glu_b": row(w["s5_glu_b"][l]),
        "w_out": w["w_out"][l].astype(BF),
        "norm_xattn_w": row(w["norm_xattn_w"][l]),
        "norm_mem_w": row(w["norm_mem_w"][l]),
        "xq": (w["xattn_w_q"][l] * qscale).astype(BF),
        "xkv": w["xattn_w_kv"][l].astype(BF),
        "xo": w["xattn_w_o"][l].astype(BF),
        "norm_ffn_w": row(w["norm_ffn_w"][l]),
        "up_a": w["ffn_w_up"][l][:, :D_FF].reshape(D_MODEL, FF_NCHUNK, FF_CHUNK).transpose(1, 0, 2).astype(BF),
        "up_g": w["ffn_w_up"][l][:, D_FF:].reshape(D_MODEL, FF_NCHUNK, FF_CHUNK).transpose(1, 0, 2).astype(BF),
        "conv_w": w["ffn_conv_w"][l].astype(F32).reshape(3, FF_NCHUNK, FF_CHUNK).transpose(1, 0, 2),
        "conv_b": w["ffn_conv_b"][l].astype(F32).reshape(FF_NCHUNK, 1, FF_CHUNK),
        "down": w["ffn_w_down"][l].astype(BF),
    }
    g = w["mix_out_norm_w"][l]
    p["g_a"], p["g_b"], p["g_c"], p["g_d"] = (row(g[i * MIXW:(i + 1) * MIXW]) for i in range(4))
    sm = jax.nn.softmax(w["hgrn_lower_bound"].astype(F32), axis=0)
    lb = jnp.concatenate([jnp.zeros_like(sm[:1]), jnp.cumsum(sm, axis=0)[:-1]], axis=0)[l]
    p["lb_tab"] = jnp.concatenate(
        [row(jnp.log(lb)), row(jnp.log1p(-lb)), row(1.0 - lb), jnp.zeros((SUBLANES - 3, MIXW), F32)], axis=0)
    return p


def _tiles(S):
    return {"tm": 512, "s5_l": 256, "tq": 256, "tk": 512, "hg_tb": 512}


def _encode(x, mem, layers, final_w):
    B, S, _ = x.shape
    t = _tiles(S)
    cos, sin = _rope_tables(S)
    depth = len(layers)
    for l, p in enumerate(layers):
        pp = dict(p, cos=cos, sin=sin)
        batch_rows = B % SUBLANES == 0
        qkva, ub, qc, kc, vct, dd = _in_proj(x, pp, t["tm"], batch_rows)
        ma = _na(qkva, p["na_bias"], p["g_a"])
        if batch_rows:
            ub = ub.reshape(B // SUBLANES, S * SUBLANES, MIXW)
        mb = _s5(ub, p["s5_f"], p["s5_b"], p["s5_d"], p["s5_glu_w"], p["s5_glu_b"], p["g_b"], t["s5_l"],
                 batch_rows)
        if batch_rows:
            mb = mb.reshape(B // SUBLANES, S, SUBLANES * MIXW)
        mc = _gqa(qc, kc, vct, p["g_c"], t["tq"], t["tk"])
        md = _hgrn(dd, p["lb_tab"], p["g_d"], p["ones"], t["hg_tb"])
        kv = _mem_kv(mem, p["norm_mem_w"], p["xkv"])
        x = _post(x, ma, mb, mc, md, kv, p, t["tm"])
        x = _ffn(x, p, final_w, t["tm"], final=(l == depth - 1))
    return x


def kernel(x_prompt, x_sample, mem_prompt, mem_sample, norm_mix_w, w_in, na_rpb, s5_lambda_re, s5_lambda_im, s5_log_dt, s5_b_re, s5_b_im, s5_c_re, s5_c_im, s5_d, s5_glu_w, s5_glu_b, gqa_q_norm_w, gqa_k_norm_w, hgrn_lower_bound, mix_out_norm_w, w_out, norm_xattn_w, norm_mem_w, xattn_w_q, xattn_w_kv, xattn_w_o, norm_ffn_w, ffn_w_up, ffn_conv_w, ffn_conv_b, ffn_w_down, final_norm_w):
    w = dict(norm_mix_w=norm_mix_w, w_in=w_in, na_rpb=na_rpb, s5_lambda_re=s5_lambda_re,
             s5_lambda_im=s5_lambda_im, s5_log_dt=s5_log_dt, s5_b_re=s5_b_re, s5_b_im=s5_b_im,
             s5_c_re=s5_c_re, s5_c_im=s5_c_im, s5_d=s5_d, s5_glu_w=s5_glu_w, s5_glu_b=s5_glu_b,
             gqa_q_norm_w=gqa_q_norm_w, gqa_k_norm_w=gqa_k_norm_w, hgrn_lower_bound=hgrn_lower_bound,
             mix_out_norm_w=mix_out_norm_w, w_out=w_out, norm_xattn_w=norm_xattn_w, norm_mem_w=norm_mem_w,
             xattn_w_q=xattn_w_q, xattn_w_kv=xattn_w_kv, xattn_w_o=xattn_w_o, norm_ffn_w=norm_ffn_w,
             ffn_w_up=ffn_w_up, ffn_conv_w=ffn_conv_w, ffn_conv_b=ffn_conv_b, ffn_w_down=ffn_w_down)
    layers = [_layer_params(l, w) for l in range(w_in.shape[0])]
    final_w = final_norm_w.reshape(1, -1).astype(F32)
    return (_encode(x_prompt, mem_prompt, layers, final_w), _encode(x_sample, mem_sample, layers, final_w))
```

```python
import functools
import math

import numpy as np
import jax
import jax.numpy as jnp
from jax import lax
from jax.experimental import pallas as pl
from jax.experimental.pallas import tpu as pltpu

D_MODEL = 1024
GRID_W = 64
HEAD_DIM = 64
EPS = 1e-6
NA_HEADS = 4
NA_WIN_ROWS = 8
NA_WIN_COLS = 16
S5_WIDTH = 256
S5_GROUP = 16
S5_GROUPS = 16
S5_STATE = 64
S5_NSTATE = S5_GROUPS * S5_STATE
GQA_Q_HEADS = 4
GQA_KV_HEADS = 2
ROPE_THETA = 10000.0
ROPE_AXIS_DIM = HEAD_DIM // 2
HGRN_HEADS = 4
XA_HEADS = 4
MEM_TOKENS = 256
D_FF = 2816
MIXW = 256
NEG = -1e30
LOG2E = 1.4426950408889634

SUBLANES = 8
FF_CHUNK = 256
FF_NCHUNK = D_FF // FF_CHUNK
HALO = 16
HG_CHUNK = 64
HG_LEVELS = (32, 16, 8)
HG_SEQS = 2
VMEM_LIMIT = 56 * 1024 * 1024

BF = jnp.bfloat16
F32 = jnp.float32


def _cparams(sem):
    return pltpu.CompilerParams(dimension_semantics=sem, vmem_limit_bytes=VMEM_LIMIT)


def _dot(a, b):
    return jnp.dot(a, b, preferred_element_type=F32)


def _dot_nt(a, b):
    return lax.dot_general(a, b, (((1,), (1,)), ((), ())), preferred_element_type=F32)


def _dot_tn(a, b):
    return lax.dot_general(a, b, (((0,), (0,)), ((), ())), preferred_element_type=F32)


def _split3(x):
    h1 = x.astype(BF)
    r1 = x - h1.astype(F32)
    h2 = r1.astype(BF)
    r2 = r1 - h2.astype(F32)
    return h1, h2, r2.astype(BF)


def _rms(x, w):
    ms = jnp.mean(x * x, axis=-1, keepdims=True)
    return x * lax.rsqrt(ms + EPS) * w


def _head_mask(shape, axis, h):
    idx = lax.broadcasted_iota(jnp.int32, shape, axis)
    return (idx >= h * HEAD_DIM) & (idx < (h + 1) * HEAD_DIM)


def _const_spec(shape):
    nd = len(shape)
    return pl.BlockSpec(shape, lambda *_: (0,) * nd)


def _in_proj_kernel(x_ref, nw_ref, wa_ref, wb_ref, wq_ref, wk_ref, wvt_ref, wd_ref, ones_ref,
                    gq_ref, gk_ref, cos_ref, sin_ref,
                    qkva_ref, ub_ref, qc_ref, kc_ref, vct_ref, dd_ref):
    x = x_ref[...]
    h = _rms(x, nw_ref[...]).astype(BF)
    ones = ones_ref[...]
    cos = cos_ref[...]
    sin = sin_ref[...]

    def norm_rope(y, g, width):
        yy = y * y
        hi = yy.astype(BF)
        lo = (yy - hi.astype(F32)).astype(BF)
        ms = (_dot(hi, ones[:width, :width]) + _dot(lo, ones[:width, :width])) * (1.0 / HEAD_DIM)
        yn = y * lax.rsqrt(ms + EPS) * g
        lane = lax.broadcasted_iota(jnp.int32, yn.shape, 1)
        first_half = (lane % HEAD_DIM) < ROPE_AXIS_DIM
        swapped = jnp.where(first_half,
                            pltpu.roll(yn, width - ROPE_AXIS_DIM, axis=1),
                            pltpu.roll(yn, ROPE_AXIS_DIM, axis=1))
        return yn * cos[:, :width] + swapped * sin[:, :width]

    q = norm_rope(_dot(h, wq_ref[...]), gq_ref[...], 4 * HEAD_DIM)
    qc_ref[...] = (q * (LOG2E / math.sqrt(HEAD_DIM))).astype(BF)
    k = norm_rope(_dot(h, wk_ref[...]), gk_ref[...][:, :2 * HEAD_DIM], 2 * HEAD_DIM)
    kc_ref[...] = k.astype(BF)
    qkva_ref[...] = _dot(h, wa_ref[...]).astype(BF)
    ub_ref[...] = _dot(h, wb_ref[...])
    dd_ref[...] = _dot(h, wd_ref[...])
    vct_ref[...] = _dot_nt(wvt_ref[...], h).astype(BF)


def _in_proj(x, p, tm):
    B, S, _ = x.shape
    grid = (B, S // tm)
    tok = lambda w: pl.BlockSpec((None, tm, w), lambda b, j: (b, j, 0))
    tab = pl.BlockSpec((tm, 4 * HEAD_DIM), lambda b, j: (j, 0))
    ins = [x, p["norm_mix_w"], p["wa"], p["wb"], p["wq"], p["wk"], p["wvt"], p["wd"], p["ones"],
           p["gq"], p["gk"], p["cos"], p["sin"]]
    in_specs = [tok(D_MODEL)] + [_const_spec(a.shape) for a in ins[1:11]] + [tab, tab]
    out_shape = (
        jax.ShapeDtypeStruct((B, S, 768), BF),
        jax.ShapeDtypeStruct((B, S, 256), F32),
        jax.ShapeDtypeStruct((B, S, 256), BF),
        jax.ShapeDtypeStruct((B, S, 128), BF),
        jax.ShapeDtypeStruct((B, 128, S), BF),
        jax.ShapeDtypeStruct((B, S, 1280), F32),
    )
    out_specs = (tok(768), tok(256), tok(256), tok(128),
                 pl.BlockSpec((None, 128, tm), lambda b, j: (b, 0, j)), tok(1280))
    return pl.pallas_call(
        _in_proj_kernel, grid=grid, in_specs=in_specs, out_specs=out_specs, out_shape=out_shape,
        compiler_params=_cparams(("parallel", "parallel")), name="in_proj")(*ins)


NA_CHUNK_ROWS = 8
NA_CHUNK = NA_CHUNK_ROWS * GRID_W
NA_KEYS = NA_WIN_ROWS * GRID_W


def _na_kernel(q_ref, kp_ref, kc_ref, kn_ref, vp_ref, vc_ref, vn_ref, bias_ref, g_ref, o_ref,
               kbuf, vbuf, *, rows):
    j = pl.program_id(1)
    kbuf[0:NA_CHUNK, :] = kp_ref[...]
    kbuf[NA_CHUNK:2 * NA_CHUNK, :] = kc_ref[...]
    kbuf[2 * NA_CHUNK:3 * NA_CHUNK, :] = kn_ref[...]
    for part, ref in enumerate((vp_ref, vc_ref, vn_ref)):
        v = ref[...]
        for h in range(NA_HEADS):
            vbuf[h, part * NA_CHUNK:(part + 1) * NA_CHUNK, :] = jnp.where(
                _head_mask(v.shape, 1, h), v, jnp.zeros_like(v))
    gain = g_ref[...]

    def row_body(rr, carry):
        r = j * NA_CHUNK_ROWS + rr
        r0 = jnp.clip(r - NA_WIN_ROWS // 2, 0, rows - NA_WIN_ROWS)
        off = pl.multiple_of((r0 - j * NA_CHUNK_ROWS + NA_CHUNK_ROWS) * GRID_W, GRID_W)
        variant = r - r0
        qrow = pl.multiple_of(rr * GRID_W, GRID_W)
        q = q_ref[pl.ds(qrow, GRID_W), :]
        kw = kbuf[pl.ds(off, NA_KEYS), :]
        qstack = jnp.concatenate(
            [jnp.where(_head_mask(q.shape, 1, h), q, jnp.zeros_like(q)) for h in range(NA_HEADS)], axis=0)
        s = _dot_nt(qstack, kw) + bias_ref[variant]
        m = jnp.max(s, axis=-1, keepdims=True)
        e = jnp.exp2(s - m)
        p = (e * (1.0 / jnp.sum(e, axis=-1, keepdims=True))).astype(BF)
        acc = _dot(p[0:GRID_W], vbuf[0, pl.ds(off, NA_KEYS), :])
        for h in range(1, NA_HEADS):
            acc = acc + _dot(p[h * GRID_W:(h + 1) * GRID_W], vbuf[h, pl.ds(off, NA_KEYS), :])
        o_ref[pl.ds(qrow, GRID_W), :] = _rms(acc, gain).astype(BF)
        return carry

    lax.fori_loop(0, NA_CHUNK_ROWS, row_body, 0, unroll=True)


def _na_bias_table(rpb):
    qcols = np.arange(GRID_W)
    kcols = np.arange(GRID_W)
    c0 = np.clip(qcols - NA_WIN_COLS // 2, 0, GRID_W - NA_WIN_COLS)
    in_win = (kcols[None, :] >= c0[:, None]) & (kcols[None, :] < c0[:, None] + NA_WIN_COLS)
    dc = np.clip(kcols[None, :] - qcols[:, None], -(NA_WIN_COLS - 1), NA_WIN_COLS - 1) + (NA_WIN_COLS - 1)
    onehot = jnp.asarray(dc[None, :, :] == np.arange(2 * NA_WIN_COLS - 1)[:, None, None], F32)
    cols = jnp.einsum('hrc,cqk->hrqk', rpb, onehot, precision=lax.Precision.HIGHEST) * LOG2E
    cols = jnp.where(in_win[None, None], cols, NEG)
    variants = []
    for var in range(NA_WIN_ROWS):
        lo = NA_WIN_ROWS - 1 - var
        bias = cols[:, lo:lo + NA_WIN_ROWS]
        variants.append(bias.transpose(0, 2, 1, 3).reshape(NA_HEADS * GRID_W, NA_KEYS))
    return jnp.stack(variants)


def _na(qkva, bias, gain):
    B, S, _ = qkva.shape
    rows = S // GRID_W
    nchunk = S // NA_CHUNK
    blk = lambda col, f: pl.BlockSpec((None, NA_CHUNK, MIXW), lambda b, j: (b, f(j), col))
    prev = lambda j: jnp.maximum(j - 1, 0)
    cur = lambda j: j
    nxt = lambda j: jnp.minimum(j + 1, nchunk - 1)
    in_specs = [blk(0, cur), blk(1, prev), blk(1, cur), blk(1, nxt), blk(2, prev), blk(2, cur), blk(2, nxt),
                _const_spec(bias.shape), _const_spec(gain.shape)]
    return pl.pallas_call(
        functools.partial(_na_kernel, rows=rows), grid=(B, nchunk), in_specs=in_specs,
        out_specs=pl.BlockSpec((None, NA_CHUNK, MIXW), lambda b, j: (b, j, 0)),
        out_shape=jax.ShapeDtypeStruct((B, S, MIXW), BF),
        scratch_shapes=[pltpu.VMEM((3 * NA_CHUNK, MIXW), BF), pltpu.VMEM((NA_HEADS, 3 * NA_CHUNK, MIXW), BF)],
        compiler_params=_cparams(("parallel", "parallel")), name="na")(
            qkva, qkva, qkva, qkva, qkva, qkva, qkva, bias, gain)


S5_LANES = 256


def _s5_kernel(*refs, rev, ntile, batch_rows):
    if rev:
        (u_ref, yf_ref, bbd_ref, cc_ref, tab_ref, d_ref, gw_ref, gb_ref, g_ref, o_ref, z_ref, carry_ref) = refs
    else:
        (u_ref, bbd_ref, cc_ref, tab_ref, o_ref, z_ref, carry_ref) = refs
    j = pl.program_id(1)

    @pl.when(j == 0)
    def _():
        carry_ref[...] = jnp.zeros_like(carry_ref)

    if batch_rows:
        u3 = u_ref[...]
        u = jnp.swapaxes(u3, 0, 1).reshape(u3.shape[0] * u3.shape[1], u3.shape[2])
    else:
        u = u_ref[...]
    ub = u.astype(BF)
    y = None
    for c0 in range(0, S5_NSTATE, S5_LANES):
        re = slice(c0, c0 + S5_LANES)
        im = slice(S5_NSTATE + c0, S5_NSTATE + c0 + S5_LANES)
        z_ref[:, re] = _dot(ub, bbd_ref[:, re])
        z_ref[:, im] = _dot(ub, bbd_ref[:, im])
        steps = [(tab_ref[2 * i, :, re], tab_ref[2 * i + 1, :, re], 1 << i) for i in range(3)]
        pr, pi = tab_ref[6, :, re], tab_ref[7, :, re]
        cr, ci = carry_ref[:, re], carry_ref[:, im]
        last = 0 if rev else SUBLANES - 1
        if batch_rows:
            first = SUBLANES - 1 - last
            pr = jnp.broadcast_to(pr[first:first + 1, :], pr.shape)
            pi = jnp.broadcast_to(pi[first:first + 1, :], pi.shape)
        for i in range(ntile):
            t = (ntile - 1 - i) if rev else i
            rows = slice(t * SUBLANES, (t + 1) * SUBLANES)
            zr = z_ref[rows, re]
            zi = z_ref[rows, im]
            if not batch_rows:
                for ar, ai, sh in steps:
                    amt = SUBLANES - sh if rev else sh
                    sr = pltpu.roll(zr, amt, axis=0)
                    si = pltpu.roll(zi, amt, axis=0)
                    zr, zi = zr + ar * sr - ai * si, zi + ar * si + ai * sr
            zr, zi = zr + pr * cr - pi * ci, zi + pr * ci + pi * cr
            z_ref[rows, re] = zr
            z_ref[rows, im] = zi
            if batch_rows:
                cr, ci = zr, zi
            else:
                cr = jnp.broadcast_to(zr[last:last + 1, :], zr.shape)
                ci = jnp.broadcast_to(zi[last:last + 1, :], zi.shape)
        carry_ref[:, re] = cr
        carry_ref[:, im] = ci
        part = _dot(z_ref[:, re].astype(BF), cc_ref[re, :]) + _dot(z_ref[:, im].astype(BF), cc_ref[im, :])
        y = part if y is None else y + part
    if not rev:
        o_ref[...] = y
    else:
        y = y + yf_ref[...] + d_ref[...] * u
        hcur = 0.5 * y * (1.0 + jnp.tanh(math.sqrt(2.0 / math.pi) * (y + 0.044715 * (y * y * y))))
        gate = _dot(hcur.astype(BF), gw_ref[...]) + gb_ref[...]
        out = _rms(hcur * (1.0 / (1.0 + jnp.exp(-gate))), g_ref[...])
        if batch_rows:
            nseq, steps, width = o_ref.shape
            out = jnp.swapaxes(out.reshape(steps, nseq, width), 0, 1)
        o_ref[...] = out.astype(BF)


def _s5_tables(lam_re, lam_im, log_dt, b_re, b_im, c_re, c_im, rev):
    lre = jnp.minimum(lam_re.astype(F32), -1e-4)
    lim = lam_im.astype(F32)
    dt = jnp.exp(log_dt.astype(F32))[:, None]
    mag = jnp.exp(lre * dt)
    ar = mag * jnp.cos(lim * dt)
    ai = mag * jnp.sin(lim * dt)
    den = lre * lre + lim * lim
    nre = ar - 1.0
    coef_re = (nre * lre + ai * lim) / den
    coef_im = (ai * lre - nre * lim) / den
    bre, bim = b_re.astype(F32), b_im.astype(F32)
    bbar_re = coef_re[..., None] * bre - coef_im[..., None] * bim
    bbar_im = coef_re[..., None] * bim + coef_im[..., None] * bre
    eye = jnp.eye(S5_GROUPS, dtype=F32)
    bd = lambda m: jnp.einsum('gnc,gh->gchn', m, eye).reshape(S5_WIDTH, S5_NSTATE)
    bbd = jnp.concatenate([bd(bbar_re), bd(bbar_im)], axis=1).astype(BF)
    cd = lambda m: jnp.einsum('gcn,gh->gnhc', m.astype(F32), eye).reshape(S5_NSTATE, S5_WIDTH)
    cc = jnp.concatenate([cd(c_re), -cd(c_im)], axis=0).astype(BF)
    a1r, a1i = ar.reshape(1, -1), ai.reshape(1, -1)
    cmul = lambda xr, xi, yr, yi: (xr * yr - xi * yi, xr * yi + xi * yr)
    a2r, a2i = cmul(a1r, a1i, a1r, a1i)
    a4r, a4i = cmul(a2r, a2i, a2r, a2i)
    pw = [(a1r, a1i)]
    for _ in range(SUBLANES - 1):
        pw.append(cmul(pw[-1][0], pw[-1][1], a1r, a1i))
    pr = jnp.concatenate([x[0] for x in pw], axis=0)
    pi = jnp.concatenate([x[1] for x in pw], axis=0)
    rows = np.arange(SUBLANES)[:, None]
    tabs = []
    for (xr, xi), sh in (((a1r, a1i), 1), ((a2r, a2i), 2), ((a4r, a4i), 4)):
        keep = jnp.asarray(rows >= sh, F32)
        tabs += [keep * xr, keep * xi]
    tabs += [pr, pi]
    tab = jnp.stack(tabs)
    if rev:
        tab = tab[:, ::-1, :]
    return bbd, cc, tab


def _s5(u, tabs_f, tabs_b, d_skip, glu_w, glu_b, gain, L, batch_rows):
    B, S, _ = u.shape
    ntile = L // SUBLANES
    scratch = [pltpu.VMEM((L, 2 * S5_NSTATE), F32), pltpu.VMEM((SUBLANES, 2 * S5_NSTATE), F32)]
    if batch_rows:
        G, steps = B // SUBLANES, L // SUBLANES
        nch = S // steps
        io = lambda f: pl.BlockSpec((SUBLANES, steps, S5_WIDTH), lambda b, j: (b, f(j), 0))
    else:
        G, nch = B, S // L
        io = lambda f: pl.BlockSpec((None, L, S5_WIDTH), lambda b, j: (b, f(j), 0))
    part = lambda f: pl.BlockSpec((None, L, S5_WIDTH), lambda b, j: (b, f(j), 0))
    fwd, bwd = (lambda j: j), (lambda j: nch - 1 - j)
    kern = functools.partial(_s5_kernel, ntile=ntile, batch_rows=batch_rows)
    yf = pl.pallas_call(
        functools.partial(kern, rev=False), grid=(G, nch),
        in_specs=[io(fwd)] + [_const_spec(a.shape) for a in tabs_f],
        out_specs=part(fwd), out_shape=jax.ShapeDtypeStruct((G, nch * L, S5_WIDTH), F32),
        scratch_shapes=scratch, compiler_params=_cparams(("parallel", "arbitrary")), name="s5_fwd")(u, *tabs_f)
    rest = (d_skip, glu_w, glu_b, gain)
    return pl.pallas_call(
        functools.partial(kern, rev=True), grid=(G, nch),
        in_specs=[io(bwd), part(bwd)] + [_const_spec(a.shape) for a in tabs_b + rest],
        out_specs=io(bwd), out_shape=jax.ShapeDtypeStruct((B, S, S5_WIDTH), BF),
        scratch_shapes=scratch, compiler_params=_cparams(("parallel", "arbitrary")), name="s5_bwd")(
            u, yf, *tabs_b, *rest)


GQA_ACC_ROWS = HEAD_DIM + 16


def _gqa_kernel(q_ref, k_ref, vt_ref, g_ref, o_ref, s_ref, p_ref, m_ref, a_ref, a2_ref, acc_ref, *, tk, nkv):
    q = q_ref[...]
    tq = q.shape[0]
    halves = (q[:, :128], q[:, 128:])
    left = lax.broadcasted_iota(jnp.int32, (tq, 128), 1) < HEAD_DIM
    zero = jnp.zeros((tq, 128), BF)
    qs = [jnp.where(left, halves[0], zero), jnp.where(left, zero, halves[0]),
          jnp.where(left, halves[1], zero), jnp.where(left, zero, halves[1])]
    m_ref[...] = jnp.full(m_ref.shape, -jnp.inf, F32)
    acc_ref[...] = jnp.zeros(acc_ref.shape, F32)
    ones_rows = jnp.ones((GQA_ACC_ROWS - HEAD_DIM, tk), BF)

    def scores(jb, slot):
        k = k_ref[pl.ds(pl.multiple_of(jb * tk, tk), tk), :]
        for i in range(4):
            st = _dot_nt(k, qs[i])
            s_ref[slot, i] = st
            m_old = m_ref[i:i + 1, :]
            m_new = jnp.maximum(m_old, jnp.max(st, axis=0, keepdims=True))
            a_ref[i:i + 1, :] = jnp.exp2(m_old - m_new)
            m_ref[i:i + 1, :] = m_new

    def probs(slot):
        for i in range(4):
            p_ref[slot, i] = jnp.exp2(s_ref[slot, i] - m_ref[i:i + 1, :]).astype(BF)
            a2_ref[i:i + 1, :] = a_ref[i:i + 1, :]

    def values(jb, slot):
        vt = vt_ref[:, pl.ds(pl.multiple_of(jb * tk, tk), tk)]
        lhs = [jnp.concatenate([vt[g * HEAD_DIM:(g + 1) * HEAD_DIM, :], ones_rows], axis=0)
               for g in range(GQA_KV_HEADS)]
        for i in range(4):
            pv = _dot(lhs[i % 2], p_ref[slot, i])
            acc_ref[i] = a2_ref[i:i + 1, :] * acc_ref[i] + pv

    scores(0, 0)
    probs(0)
    scores(1, 1)

    def stage_triple(jb, slot):
        values(jb, slot)
        probs(1 - slot)
        scores(jb + 2, slot)

    def kv_body(jj, carry):
        stage_triple(2 * jj, 0)
        stage_triple(2 * jj + 1, 1)
        return carry

    lax.fori_loop(0, (nkv - 2) // 2, kv_body, 0)
    values(nkv - 2, (nkv - 2) % 2)
    probs((nkv - 1) % 2)
    values(nkv - 1, (nkv - 1) % 2)
    outs = [acc_ref[i, 0:HEAD_DIM, :] * (1.0 / acc_ref[i, HEAD_DIM:HEAD_DIM + 1, :]) for i in range(4)]
    ot = jnp.concatenate([outs[0], outs[2], outs[1], outs[3]], axis=0)
    o_ref[...] = _rms(ot.T, g_ref[...]).astype(BF)


def _gqa(qc, kc, vct, gain, tq, tk):
    B, S, _ = qc.shape
    assert S % (2 * tk) == 0 and S // tk >= 2, "the kv loop handles two blocks per trip"
    return pl.pallas_call(
        functools.partial(_gqa_kernel, tk=tk, nkv=S // tk), grid=(B, S // tq),
        in_specs=[pl.BlockSpec((None, tq, 256), lambda b, j: (b, j, 0)),
                  pl.BlockSpec((None, S, 128), lambda b, j: (b, 0, 0)),
                  pl.BlockSpec((None, 128, S), lambda b, j: (b, 0, 0)),
                  _const_spec(gain.shape)],
        out_specs=pl.BlockSpec((None, tq, MIXW), lambda b, j: (b, j, 0)),
        out_shape=jax.ShapeDtypeStruct((B, S, MIXW), BF),
        scratch_shapes=[pltpu.VMEM((2, 4, tk, tq), F32), pltpu.VMEM((2, 4, tk, tq), BF),
                        pltpu.VMEM((SUBLANES, tq), F32), pltpu.VMEM((SUBLANES, tq), F32),
                        pltpu.VMEM((SUBLANES, tq), F32), pltpu.VMEM((4, GQA_ACC_ROWS, tq), F32)],
        compiler_params=_cparams(("parallel", "parallel")), name="gqa")(qc, kc, vct, gain)


def _hgrn_level_masks(rev):
    t = np.arange(HG_CHUNK)[:, None]
    s = np.arange(HG_CHUNK)[None, :]
    out = []
    for m in HG_LEVELS:
        bt, bs = t // m, s // m
        ok = ((bt % 2 == 0) & (bs == bt + 1)) if rev else ((bt % 2 == 1) & (bs == bt - 1))
        out.append(np.tile(ok.astype(np.float32), (1, HGRN_HEADS)))
    for dlt in range(SUBLANES):
        ok = (s == (t + dlt if rev else t - dlt)) & (s // SUBLANES == t // SUBLANES)
        out.append(np.tile(ok.astype(np.float32), (1, HGRN_HEADS)))
    return np.stack(out)


def _hgrn_kernel(*refs, rev, nchunk, nseq):
    if rev:
        (q_ref, z_ref, v_ref, gate_ref, of_ref, lb_ref, tri_ref, ones_ref, lmask_ref, g_ref,
         o_ref, st_ref) = refs
    else:
        (q_ref, z_ref, v_ref, lb_ref, tri_ref, ones_ref, lmask_ref, o_ref, st_ref) = refs
    C = HG_CHUNK
    j = pl.program_id(1)

    @pl.when(j == 0)
    def _():
        st_ref[...] = jnp.zeros_like(st_ref)

    log_lb = lb_ref[0:1, :]
    log1m_lb = lb_ref[1:2, :]
    one_m_lb = lb_ref[2:3, :]
    tri = tri_ref[...]
    ones = ones_ref[...]
    row_in_tile = lax.broadcasted_iota(jnp.int32, (C, MIXW), 0) % SUBLANES
    bd_mask = (lax.broadcasted_iota(jnp.int32, (MIXW, MIXW), 0) // HEAD_DIM
               == lax.broadcasted_iota(jnp.int32, (MIXW, MIXW), 1) // HEAD_DIM)
    zero_row = jnp.zeros((1, MIXW), F32)

    def tile_roll(x, amt):
        return jnp.concatenate(
            [pltpu.roll(x[r:r + SUBLANES, :], amt, axis=0) for r in range(0, C, SUBLANES)], axis=0)

    def stack_heads(x):
        xb = x.astype(BF)
        return jnp.concatenate(
            [jnp.where(_head_mask(xb.shape, 1, h), xb, jnp.zeros_like(xb)) for h in range(HGRN_HEADS)], axis=0)

    def chunk_body(ci, carry):
        for si in range(nseq):
            seq_chunk(ci, si)
        return carry

    def seq_chunk(ci, si):
        c = (nchunk - 1 - ci) if rev else ci
        rows = pl.ds(pl.multiple_of(c * C, C), C)
        q = q_ref[si, rows, :]
        z = z_ref[si, rows, :]
        v = v_ref[si, rows, :]
        ls = jnp.minimum(z, 0.0) - jnp.log(1.0 + jnp.exp(-jnp.abs(z)))
        bv = log1m_lb + ls
        logf = jnp.maximum(log_lb, bv) + jnp.log(1.0 + jnp.exp(-jnp.abs(log_lb - bv)))
        kin = one_m_lb * jnp.exp(ls - z)
        pieces = _dot(tri, jnp.concatenate(_split3(logf * LOG2E), axis=1))
        b = pieces[:, 0:MIXW] + pieces[:, MIXW:2 * MIXW] + pieces[:, 2 * MIXW:3 * MIXW]
        edge = b[0:1, :] if rev else b[C - 1:C, :]

        st = st_ref[si]
        o = _dot_nt((q * jnp.exp2(b)).astype(BF), st.astype(BF))

        att = jnp.zeros((C, HGRN_HEADS * C), F32)
        for li, m in enumerate(HG_LEVELS):
            nb = C // m
            qrefs, krefs = [], []
            for blk in range(nb):
                if rev:
                    qrow = (blk + 1) * m
                    krow = blk * m
                else:
                    qrow = blk * m - 1
                    krow = (blk + 1) * m - 1
                qr = zero_row if (qrow < 0 or qrow >= C) else b[qrow:qrow + 1, :]
                qrefs.append(jnp.broadcast_to(qr, (m, MIXW)))
                krefs.append(jnp.broadcast_to(b[krow:krow + 1, :], (m, MIXW)))
            bq = jnp.concatenate(qrefs, axis=0)
            bk = jnp.concatenate(krefs, axis=0)
            ql = (q * jnp.exp2(b - bq)).astype(BF)
            kl = kin * jnp.exp2(bk - b)
            att = att + lmask_ref[li] * _dot_nt(ql, stack_heads(kl))
        ds = [(q * kin).astype(BF)]
        for dlt in range(1, SUBLANES):
            amt = SUBLANES - dlt if rev else dlt
            valid = (row_in_tile + dlt <= SUBLANES - 1) if rev else (row_in_tile >= dlt)
            ks = tile_roll(kin, amt)
            bs = tile_roll(b, amt)
            ds.append((q * ks * jnp.exp2(jnp.where(valid, b - bs, 0.0))).astype(BF))
        hsum = _dot(jnp.concatenate(ds, axis=0), ones)
        for dlt in range(SUBLANES):
            att = att + lmask_ref[len(HG_LEVELS) + dlt] * hsum[dlt * C:(dlt + 1) * C, :]
        o = o + _dot(att.astype(BF), stack_heads(v))

        kb = (kin * jnp.exp2(edge - b)).astype(BF)
        upd = _dot_tn(v.astype(BF), kb)
        st_ref[si] = st * jnp.exp2(edge) + jnp.where(bd_mask, upd, 0.0)

        if rev:
            tot = o + of_ref[si, rows, :]
            gate = gate_ref[si, rows, :]
            silu = gate * (1.0 / (1.0 + jnp.exp(-gate)))
            o_ref[si, rows, :] = (_rms(tot, g_ref[...]) * silu).astype(BF)
        else:
            o_ref[si, rows, :] = o

    lax.fori_loop(0, nchunk, chunk_body, 0, unroll=True)


def _hgrn(dd, lb_tab, gain, tb):
    B, S, _ = dd.shape
    ones = jnp.asarray(np.arange(MIXW)[:, None] // HEAD_DIM == np.arange(HGRN_HEADS * HG_CHUNK)[None, :] // HG_CHUNK,
                       BF)
    nblk = S // tb
    nchunk = tb // HG_CHUNK
    tri_f = jnp.asarray(np.tril(np.ones((HG_CHUNK, HG_CHUNK), np.float32)), BF)
    tri_b = jnp.asarray(np.triu(np.ones((HG_CHUNK, HG_CHUNK), np.float32)), BF)
    lm_f = jnp.asarray(_hgrn_level_masks(False))
    lm_b = jnp.asarray(_hgrn_level_masks(True))
    nseq = HG_SEQS
    assert B % nseq == 0
    fcol = lambda col: pl.BlockSpec((nseq, tb, MIXW), lambda b, j: (b, j, col))
    bcol = lambda col: pl.BlockSpec((nseq, tb, MIXW), lambda b, j: (b, nblk - 1 - j, col))
    scratch = [pltpu.VMEM((nseq, MIXW, MIXW), F32)]
    consts_f = (lb_tab, tri_f, ones, lm_f)
    of = pl.pallas_call(
        functools.partial(_hgrn_kernel, rev=False, nchunk=nchunk, nseq=nseq), grid=(B // nseq, nblk),
        in_specs=[fcol(0), fcol(1), fcol(3)] + [_const_spec(a.shape) for a in consts_f],
        out_specs=fcol(0), out_shape=jax.ShapeDtypeStruct((B, S, MIXW), F32),
        scratch_shapes=scratch, compiler_params=_cparams(("parallel", "arbitrary")), name="hgrn_fwd")(
            dd, dd, dd, *consts_f)
    consts_b = (lb_tab, tri_b, ones, lm_b, gain)
    return pl.pallas_call(
        functools.partial(_hgrn_kernel, rev=True, nchunk=nchunk, nseq=nseq), grid=(B // nseq, nblk),
        in_specs=[bcol(0), bcol(2), bcol(3), bcol(4), bcol(0)] + [_const_spec(a.shape) for a in consts_b],
        out_specs=bcol(0), out_shape=jax.ShapeDtypeStruct((B, S, MIXW), BF),
        scratch_shapes=scratch, compiler_params=_cparams(("parallel", "arbitrary")), name="hgrn_bwd")(
            dd, dd, dd, dd, of, *consts_b)


def _mem_kv_kernel(mem_ref, nw_ref, wkv_ref, o_ref):
    o_ref[...] = _dot(_rms(mem_ref[...], nw_ref[...]).astype(BF), wkv_ref[...]).astype(BF)


def _mem_kv(mem, nw, wkv):
    B, M, _ = mem.shape
    return pl.pallas_call(
        _mem_kv_kernel, grid=(B,),
        in_specs=[pl.BlockSpec((None, M, D_MODEL), lambda b: (b, 0, 0)), _const_spec(nw.shape),
                  _const_spec(wkv.shape)],
        out_specs=pl.BlockSpec((None, M, 2 * MIXW), lambda b: (b, 0, 0)),
        out_shape=jax.ShapeDtypeStruct((B, M, 2 * MIXW), BF),
        compiler_params=_cparams(("parallel",)), name="mem_kv")(mem, nw, wkv)


def _post_kernel(x_ref, ma_ref, mb_ref, mc_ref, md_ref, wo_ref, nx_ref, wq_ref, kv_ref, wxo_ref, o_ref):
    x1 = x_ref[...]
    for i, m_ref in enumerate((ma_ref, mb_ref, mc_ref, md_ref)):
        x1 = x1 + _dot(m_ref[...], wo_ref[i * MIXW:(i + 1) * MIXW, :])
    hq = _rms(x1, nx_ref[...]).astype(BF)
    q = _dot(hq, wq_ref[...]).astype(BF)
    kmem = kv_ref[:, :MIXW]
    vmem = kv_ref[:, MIXW:]
    acc = jnp.zeros(q.shape, F32)
    for h in range(XA_HEADS):
        qm = jnp.where(_head_mask(q.shape, 1, h), q, jnp.zeros_like(q))
        s = _dot_nt(qm, kmem)
        m = jnp.max(s, axis=-1, keepdims=True)
        e = jnp.exp(s - m)
        p = (e * (1.0 / jnp.sum(e, axis=-1, keepdims=True))).astype(BF)
        vm = jnp.where(_head_mask(vmem.shape, 1, h), vmem, jnp.zeros_like(vmem))
        acc = acc + _dot(p, vm)
    o_ref[...] = x1 + _dot(acc.astype(BF), wxo_ref[...])


def _post(x, ma, mb, mc, md, kv, p, tm):
    B, S, _ = x.shape
    tok = lambda w: pl.BlockSpec((None, tm, w), lambda b, j: (b, j, 0))
    consts = (p["w_out"], p["norm_xattn_w"], p["xq"])
    return pl.pallas_call(
        _post_kernel, grid=(B, S // tm),
        in_specs=[tok(D_MODEL), tok(MIXW), tok(MIXW), tok(MIXW), tok(MIXW)]
        + [_const_spec(a.shape) for a in consts]
        + [pl.BlockSpec((None, MEM_TOKENS, 2 * MIXW), lambda b, j: (b, 0, 0)), _const_spec(p["xo"].shape)],
        out_specs=tok(D_MODEL), out_shape=jax.ShapeDtypeStruct(x.shape, F32),
        compiler_params=_cparams(("parallel", "parallel")), name="post")(
            x, ma, mb, mc, md, *consts, kv, p["xo"])


def _ffn_kernel(xp_ref, xc_ref, xn_ref, nw_ref, wa_ref, wg_ref, cw_ref, cb_ref, wd_ref, fw_ref, o_ref,
                hn_ref, gx_ref, a_ref, act_ref, *, tm, final):
    j = pl.program_id(1)
    nj = pl.num_programs(1)
    nw = nw_ref[...]
    hp = _rms(xp_ref[...], nw)
    hnx = _rms(xn_ref[...], nw)
    hn_ref[0:HALO, :] = jnp.where(j > 0, hp, 0.0).astype(BF)
    hn_ref[HALO:HALO + tm, :] = _rms(xc_ref[...], nw).astype(BF)
    hn_ref[HALO + tm:2 * HALO + tm, :] = jnp.where(j < nj - 1, hnx, 0.0).astype(BF)

    def up(c, slot):
        a_ref[slot] = _dot(hn_ref[HALO:HALO + tm, :], wa_ref[c])
        gx_ref[slot] = _dot(hn_ref[...], wg_ref[c])

    def activate(c, slot):
        cw = cw_ref[c]
        gc = (cb_ref[c] + gx_ref[slot, HALO - 1:HALO - 1 + tm, :] * cw[0:1, :]
              + gx_ref[slot, HALO:HALO + tm, :] * cw[1:2, :]
              + gx_ref[slot, HALO + 1:HALO + 1 + tm, :] * cw[2:3, :])
        act_ref[:, c * FF_CHUNK:(c + 1) * FF_CHUNK] = (
            gc * (1.0 / (1.0 + jnp.exp(-gc))) * a_ref[slot]).astype(BF)

    up(0, 0)
    for c in range(1, FF_NCHUNK):
        up(c, c % 2)
        activate(c - 1, (c - 1) % 2)
    activate(FF_NCHUNK - 1, (FF_NCHUNK - 1) % 2)
    y = xc_ref[...] + _dot(act_ref[...], wd_ref[...])
    if final:
        y = _rms(y, fw_ref[...])
    o_ref[...] = y


def _ffn(x, p, final_w, tm, final):
    B, S, _ = x.shape
    per = tm // HALO
    nh = S // HALO
    consts = (p["norm_ffn_w"], p["up_a"], p["up_g"], p["conv_w"], p["conv_b"], p["down"], final_w)
    return pl.pallas_call(
        functools.partial(_ffn_kernel, tm=tm, final=final), grid=(B, S // tm),
        in_specs=[pl.BlockSpec((None, HALO, D_MODEL), lambda b, j: (b, jnp.maximum(j * per - 1, 0), 0)),
                  pl.BlockSpec((None, tm, D_MODEL), lambda b, j: (b, j, 0)),
                  pl.BlockSpec((None, HALO, D_MODEL), lambda b, j: (b, jnp.minimum((j + 1) * per, nh - 1), 0))]
        + [_const_spec(a.shape) for a in consts],
        out_specs=pl.BlockSpec((None, tm, D_MODEL), lambda b, j: (b, j, 0)),
        out_shape=jax.ShapeDtypeStruct(x.shape, F32),
        scratch_shapes=[pltpu.VMEM((tm + 2 * HALO, D_MODEL), BF), pltpu.VMEM((2, tm + 2 * HALO, FF_CHUNK), F32),
                        pltpu.VMEM((2, tm, FF_CHUNK), F32), pltpu.VMEM((tm, D_FF), BF)],
        compiler_params=_cparams(("parallel", "parallel")), name="ffn")(x, x, x, *consts)


def _rope_tables(S):
    t = jnp.arange(S)
    inv = 1.0 / (ROPE_THETA ** (jnp.arange(0, ROPE_AXIS_DIM, 2, dtype=F32) / ROPE_AXIS_DIM))
    ang = jnp.concatenate([(t // GRID_W).astype(F32)[:, None] * inv,
                           (t % GRID_W).astype(F32)[:, None] * inv], axis=-1)
    cos, sin = jnp.cos(ang), jnp.sin(ang)
    cos_h = jnp.concatenate([cos, cos], axis=-1)
    sin_h = jnp.concatenate([-sin, sin], axis=-1)
    return jnp.tile(cos_h, (1, GQA_Q_HEADS)), jnp.tile(sin_h, (1, GQA_Q_HEADS))


def _layer_params(l, w):
    row = lambda a: a.reshape(1, -1).astype(F32)
    w_in = w["w_in"][l]
    qscale = 1.0 / math.sqrt(HEAD_DIM)
    wa = jnp.concatenate([w_in[:, :256] * (qscale * LOG2E), w_in[:, 256:768]], axis=1)
    wq = w_in[:, 1024:1280].reshape(D_MODEL, GQA_Q_HEADS, HEAD_DIM)[:, np.array([0, 2, 1, 3]), :].reshape(
        D_MODEL, 256)
    blk = np.arange(256) // HEAD_DIM
    p = {
        "norm_mix_w": row(w["norm_mix_w"][l]),
        "wa": wa.astype(BF),
        "wb": w_in[:, 768:1024].astype(BF),
        "wq": wq.astype(BF),
        "wk": w_in[:, 1280:1408].astype(BF),
        "wvt": w_in[:, 1408:1536].T.astype(BF),
        "wd": w_in[:, 1536:].astype(BF),
        "ones": jnp.asarray(blk[:, None] == blk[None, :], BF),
        "gq": row(jnp.tile(w["gqa_q_norm_w"][l], GQA_Q_HEADS)),
        "gk": row(jnp.tile(w["gqa_k_norm_w"][l], GQA_Q_HEADS)),
        "na_bias": _na_bias_table(w["na_rpb"][l].astype(F32)),
        "s5_f": _s5_tables(*[w[k][l, 0] for k in ("s5_lambda_re", "s5_lambda_im", "s5_log_dt", "s5_b_re",
                                                    "s5_b_im", "s5_c_re", "s5_c_im")], rev=False),
        "s5_b": _s5_tables(*[w[k][l, 1] for k in ("s5_lambda_re", "s5_lambda_im", "s5_log_dt", "s5_b_re",
                                                    "s5_b_im", "s5_c_re", "s5_c_im")], rev=True),
        "s5_d": row(w["s5_d"][l]),
        "s5_glu_w": w["s5_glu_w"][l].astype(BF),
        "s5_glu_b": row(w["s5_glu_b"][l]),
        "w_out": w["w_out"][l].astype(BF),
        "norm_xattn_w": row(w["norm_xattn_w"][l]),
        "norm_mem_w": row(w["norm_mem_w"][l]),
        "xq": (w["xattn_w_q"][l] * qscale).astype(BF),
        "xkv": w["xattn_w_kv"][l].astype(BF),
        "xo": w["xattn_w_o"][l].astype(BF),
        "norm_ffn_w": row(w["norm_ffn_w"][l]),
        "up_a": w["ffn_w_up"][l][:, :D_FF].reshape(D_MODEL, FF_NCHUNK, FF_CHUNK).transpose(1, 0, 2).astype(BF),
        "up_g": w["ffn_w_up"][l][:, D_FF:].reshape(D_MODEL, FF_NCHUNK, FF_CHUNK).transpose(1, 0, 2).astype(BF),
        "conv_w": w["ffn_conv_w"][l].astype(F32).reshape(3, FF_NCHUNK, FF_CHUNK).transpose(1, 0, 2),
        "conv_b": w["ffn_conv_b"][l].astype(F32).reshape(FF_NCHUNK, 1, FF_CHUNK),
        "down": w["ffn_w_down"][l].astype(BF),
    }
    g = w["mix_out_norm_w"][l]
    p["g_a"], p["g_b"], p["g_c"], p["g_d"] = (row(g[i * MIXW:(i + 1) * MIXW]) for i in range(4))
    sm = jax.nn.softmax(w["hgrn_lower_bound"].astype(F32), axis=0)
    lb = jnp.concatenate([jnp.zeros_like(sm[:1]), jnp.cumsum(sm, axis=0)[:-1]], axis=0)[l]
    p["lb_tab"] = jnp.concatenate(
        [row(jnp.log(lb)), row(jnp.log1p(-lb)), row(1.0 - lb), jnp.zeros((SUBLANES - 3, MIXW), F32)], axis=0)
    return p


def _tiles(S):
    return {"tm": 512, "s5_l": 256, "tq": 256, "tk": 512, "hg_tb": 512}


def _encode(x, mem, layers, final_w):
    B, S, _ = x.shape
    t = _tiles(S)
    cos, sin = _rope_tables(S)
    depth = len(layers)
    for l, p in enumerate(layers):
        pp = dict(p, cos=cos, sin=sin)
        qkva, ub, qc, kc, vct, dd = _in_proj(x, pp, t["tm"])
        ma = _na(qkva, p["na_bias"], p["g_a"])
        mb = _s5(ub, p["s5_f"], p["s5_b"], p["s5_d"], p["s5_glu_w"], p["s5_glu_b"], p["g_b"], t["s5_l"],
                 batch_rows=(B % SUBLANES == 0))
        mc = _gqa(qc, kc, vct, p["g_c"], t["tq"], t["tk"])
        md = _hgrn(dd, p["lb_tab"], p["g_d"], t["hg_tb"])
        kv = _mem_kv(mem, p["norm_mem_w"], p["xkv"])
        x = _post(x, ma, mb, mc, md, kv, p, t["tm"])
        x = _ffn(x, p, final_w, t["tm"], final=(l == depth - 1))
    return x


def kernel(x_prompt, x_sample, mem_prompt, mem_sample, norm_mix_w, w_in, na_rpb, s5_lambda_re, s5_lambda_im, s5_log_dt, s5_b_re, s5_b_im, s5_c_re, s5_c_im, s5_d, s5_glu_w, s5_glu_b, gqa_q_norm_w, gqa_k_norm_w, hgrn_lower_bound, mix_out_norm_w, w_out, norm_xattn_w, norm_mem_w, xattn_w_q, xattn_w_kv, xattn_w_o, norm_ffn_w, ffn_w_up, ffn_conv_w, ffn_conv_b, ffn_w_down, final_norm_w):
    w = dict(norm_mix_w=norm_mix_w, w_in=w_in, na_rpb=na_rpb, s5_lambda_re=s5_lambda_re,
             s5_lambda_im=s5_lambda_im, s5_log_dt=s5_log_dt, s5_b_re=s5_b_re, s5_b_im=s5_b_im,
             s5_c_re=s5_c_re, s5_c_im=s5_c_im, s5_d=s5_d, s5_glu_w=s5_glu_w, s5_glu_b=s5_glu_b,
             gqa_q_norm_w=gqa_q_norm_w, gqa_k_norm_w=gqa_k_norm_w, hgrn_lower_bound=hgrn_lower_bound,
             mix_out_norm_w=mix_out_norm_w, w_out=w_out, norm_xattn_w=norm_xattn_w, norm_mem_w=norm_mem_w,
             xattn_w_q=xattn_w_q, xattn_w_kv=xattn_w_kv, xattn_w_o=xattn_w_o, norm_ffn_w=norm_ffn_w,
             ffn_w_up=ffn_w_up, ffn_conv_w=ffn_conv_w, ffn_conv_b=ffn_conv_b, ffn_w_down=ffn_w_down)
    layers = [_layer_params(l, w) for l in range(w_in.shape[0])]
    final_w = final_norm_w.reshape(1, -1).astype(F32)
    return (_encode(x_prompt, mem_prompt, layers, final_w), _encode(x_sample, mem_sample, layers, final_w))
```

```python
import functools
import math

import numpy as np
import jax
import jax.numpy as jnp
from jax import lax
from jax.experimental import pallas as pl
from jax.experimental.pallas import tpu as pltpu

D_MODEL = 1024
GRID_W = 64
HEAD_DIM = 64
EPS = 1e-6
NA_HEADS = 4
NA_WIN_ROWS = 8
NA_WIN_COLS = 16
S5_WIDTH = 256
S5_GROUP = 16
S5_GROUPS = 16
S5_STATE = 64
S5_NSTATE = S5_GROUPS * S5_STATE
GQA_Q_HEADS = 4
GQA_KV_HEADS = 2
ROPE_THETA = 10000.0
ROPE_AXIS_DIM = HEAD_DIM // 2
HGRN_HEADS = 4
XA_HEADS = 4
MEM_TOKENS = 256
D_FF = 2816
MIXW = 256
NEG = -1e30
LOG2E = 1.4426950408889634

SUBLANES = 8
FF_CHUNK = 256
FF_NCHUNK = D_FF // FF_CHUNK
HALO = 16
HG_CHUNK = 64
HG_LEVELS = (32, 16, 8)
HG_SEQS = 2
VMEM_LIMIT = 56 * 1024 * 1024

BF = jnp.bfloat16
F32 = jnp.float32


def _cparams(sem):
    return pltpu.CompilerParams(dimension_semantics=sem, vmem_limit_bytes=VMEM_LIMIT)


def _dot(a, b):
    return jnp.dot(a, b, preferred_element_type=F32)


def _dot_nt(a, b):
    return lax.dot_general(a, b, (((1,), (1,)), ((), ())), preferred_element_type=F32)


def _dot_tn(a, b):
    return lax.dot_general(a, b, (((0,), (0,)), ((), ())), preferred_element_type=F32)


def _split3(x):
    h1 = x.astype(BF)
    r1 = x - h1.astype(F32)
    h2 = r1.astype(BF)
    r2 = r1 - h2.astype(F32)
    return h1, h2, r2.astype(BF)


def _rms(x, w):
    ms = jnp.mean(x * x, axis=-1, keepdims=True)
    return x * lax.rsqrt(ms + EPS) * w


def _head_mask(shape, axis, h):
    idx = lax.broadcasted_iota(jnp.int32, shape, axis)
    return (idx >= h * HEAD_DIM) & (idx < (h + 1) * HEAD_DIM)


def _const_spec(shape):
    nd = len(shape)
    return pl.BlockSpec(shape, lambda *_: (0,) * nd)


def _in_proj_kernel(x_ref, nw_ref, wa_ref, wb_ref, wq_ref, wk_ref, wvt_ref, wd_ref, ones_ref,
                    gq_ref, gk_ref, cos_ref, sin_ref,
                    qkva_ref, ub_ref, qc_ref, kc_ref, vct_ref, dd_ref):
    x = x_ref[...]
    h = _rms(x, nw_ref[...]).astype(BF)
    ones = ones_ref[...]
    cos = cos_ref[...]
    sin = sin_ref[...]

    def norm_rope(y, g, width):
        yy = y * y
        hi = yy.astype(BF)
        lo = (yy - hi.astype(F32)).astype(BF)
        ms = (_dot(hi, ones[:width, :width]) + _dot(lo, ones[:width, :width])) * (1.0 / HEAD_DIM)
        yn = y * lax.rsqrt(ms + EPS) * g
        lane = lax.broadcasted_iota(jnp.int32, yn.shape, 1)
        first_half = (lane % HEAD_DIM) < ROPE_AXIS_DIM
        swapped = jnp.where(first_half,
                            pltpu.roll(yn, width - ROPE_AXIS_DIM, axis=1),
                            pltpu.roll(yn, ROPE_AXIS_DIM, axis=1))
        return yn * cos[:, :width] + swapped * sin[:, :width]

    q = norm_rope(_dot(h, wq_ref[...]), gq_ref[...], 4 * HEAD_DIM)
    qc_ref[...] = (q * (LOG2E / math.sqrt(HEAD_DIM))).astype(BF)
    k = norm_rope(_dot(h, wk_ref[...]), gk_ref[...][:, :2 * HEAD_DIM], 2 * HEAD_DIM)
    kc_ref[...] = k.astype(BF)
    qkva_ref[...] = _dot(h, wa_ref[...]).astype(BF)
    ub_ref[...] = _dot(h, wb_ref[...])
    dd_ref[...] = _dot(h, wd_ref[...])
    vct_ref[...] = _dot_nt(wvt_ref[...], h).astype(BF)


def _in_proj(x, p, tm):
    B, S, _ = x.shape
    grid = (B, S // tm)
    tok = lambda w: pl.BlockSpec((None, tm, w), lambda b, j: (b, j, 0))
    tab = pl.BlockSpec((tm, 4 * HEAD_DIM), lambda b, j: (j, 0))
    ins = [x, p["norm_mix_w"], p["wa"], p["wb"], p["wq"], p["wk"], p["wvt"], p["wd"], p["ones"],
           p["gq"], p["gk"], p["cos"], p["sin"]]
    in_specs = [tok(D_MODEL)] + [_const_spec(a.shape) for a in ins[1:11]] + [tab, tab]
    out_shape = (
        jax.ShapeDtypeStruct((B, S, 768), BF),
        jax.ShapeDtypeStruct((B, S, 256), F32),
        jax.ShapeDtypeStruct((B, S, 256), BF),
        jax.ShapeDtypeStruct((B, S, 128), BF),
        jax.ShapeDtypeStruct((B, 128, S), BF),
        jax.ShapeDtypeStruct((B, S, 1280), F32),
    )
    out_specs = (tok(768), tok(256), tok(256), tok(128),
                 pl.BlockSpec((None, 128, tm), lambda b, j: (b, 0, j)), tok(1280))
    return pl.pallas_call(
        _in_proj_kernel, grid=grid, in_specs=in_specs, out_specs=out_specs, out_shape=out_shape,
        compiler_params=_cparams(("parallel", "parallel")), name="in_proj")(*ins)


NA_CHUNK_ROWS = 8
NA_CHUNK = NA_CHUNK_ROWS * GRID_W
NA_KEYS = NA_WIN_ROWS * GRID_W


def _na_kernel(q_ref, kp_ref, kc_ref, kn_ref, vp_ref, vc_ref, vn_ref, bias_ref, g_ref, o_ref,
               kbuf, vbuf, *, rows):
    j = pl.program_id(1)
    kbuf[0:NA_CHUNK, :] = kp_ref[...]
    kbuf[NA_CHUNK:2 * NA_CHUNK, :] = kc_ref[...]
    kbuf[2 * NA_CHUNK:3 * NA_CHUNK, :] = kn_ref[...]
    for part, ref in enumerate((vp_ref, vc_ref, vn_ref)):
        v = ref[...]
        for h in range(NA_HEADS):
            vbuf[h, part * NA_CHUNK:(part + 1) * NA_CHUNK, :] = jnp.where(
                _head_mask(v.shape, 1, h), v, jnp.zeros_like(v))
    gain = g_ref[...]

    def row_body(rr, carry):
        r = j * NA_CHUNK_ROWS + rr
        r0 = jnp.clip(r - NA_WIN_ROWS // 2, 0, rows - NA_WIN_ROWS)
        off = pl.multiple_of((r0 - j * NA_CHUNK_ROWS + NA_CHUNK_ROWS) * GRID_W, GRID_W)
        variant = r - r0
        qrow = pl.multiple_of(rr * GRID_W, GRID_W)
        q = q_ref[pl.ds(qrow, GRID_W), :]
        kw = kbuf[pl.ds(off, NA_KEYS), :]
        qstack = jnp.concatenate(
            [jnp.where(_head_mask(q.shape, 1, h), q, jnp.zeros_like(q)) for h in range(NA_HEADS)], axis=0)
        s = _dot_nt(qstack, kw) + bias_ref[variant]
        m = jnp.max(s, axis=-1, keepdims=True)
        e = jnp.exp2(s - m)
        p = (e * (1.0 / jnp.sum(e, axis=-1, keepdims=True))).astype(BF)
        acc = _dot(p[0:GRID_W], vbuf[0, pl.ds(off, NA_KEYS), :])
        for h in range(1, NA_HEADS):
            acc = acc + _dot(p[h * GRID_W:(h + 1) * GRID_W], vbuf[h, pl.ds(off, NA_KEYS), :])
        o_ref[pl.ds(qrow, GRID_W), :] = _rms(acc, gain).astype(BF)
        return carry

    lax.fori_loop(0, NA_CHUNK_ROWS, row_body, 0, unroll=True)


def _na_bias_table(rpb):
    qcols = np.arange(GRID_W)
    kcols = np.arange(GRID_W)
    c0 = np.clip(qcols - NA_WIN_COLS // 2, 0, GRID_W - NA_WIN_COLS)
    in_win = (kcols[None, :] >= c0[:, None]) & (kcols[None, :] < c0[:, None] + NA_WIN_COLS)
    dc = np.clip(kcols[None, :] - qcols[:, None], -(NA_WIN_COLS - 1), NA_WIN_COLS - 1) + (NA_WIN_COLS - 1)
    onehot = jnp.asarray(dc[None, :, :] == np.arange(2 * NA_WIN_COLS - 1)[:, None, None], F32)
    cols = jnp.einsum('hrc,cqk->hrqk', rpb, onehot, precision=lax.Precision.HIGHEST) * LOG2E
    cols = jnp.where(in_win[None, None], cols, NEG)
    variants = []
    for var in range(NA_WIN_ROWS):
        lo = NA_WIN_ROWS - 1 - var
        bias = cols[:, lo:lo + NA_WIN_ROWS]
        variants.append(bias.transpose(0, 2, 1, 3).reshape(NA_HEADS * GRID_W, NA_KEYS))
    return jnp.stack(variants)


def _na(qkva, bias, gain):
    B, S, _ = qkva.shape
    rows = S // GRID_W
    nchunk = S // NA_CHUNK
    blk = lambda col, f: pl.BlockSpec((None, NA_CHUNK, MIXW), lambda b, j: (b, f(j), col))
    prev = lambda j: jnp.maximum(j - 1, 0)
    cur = lambda j: j
    nxt = lambda j: jnp.minimum(j + 1, nchunk - 1)
    in_specs = [blk(0, cur), blk(1, prev), blk(1, cur), blk(1, nxt), blk(2, prev), blk(2, cur), blk(2, nxt),
                _const_spec(bias.shape), _const_spec(gain.shape)]
    return pl.pallas_call(
        functools.partial(_na_kernel, rows=rows), grid=(B, nchunk), in_specs=in_specs,
        out_specs=pl.BlockSpec((None, NA_CHUNK, MIXW), lambda b, j: (b, j, 0)),
        out_shape=jax.ShapeDtypeStruct((B, S, MIXW), BF),
        scratch_shapes=[pltpu.VMEM((3 * NA_CHUNK, MIXW), BF), pltpu.VMEM((NA_HEADS, 3 * NA_CHUNK, MIXW), BF)],
        compiler_params=_cparams(("parallel", "parallel")), name="na")(
            qkva, qkva, qkva, qkva, qkva, qkva, qkva, bias, gain)


S5_LANES = 256


def _s5_kernel(*refs, rev, ntile, batch_rows):
    if rev:
        (u_ref, yf_ref, bbd_ref, cc_ref, tab_ref, d_ref, gw_ref, gb_ref, g_ref, o_ref, z_ref, carry_ref) = refs
    else:
        (u_ref, bbd_ref, cc_ref, tab_ref, o_ref, z_ref, carry_ref) = refs
    j = pl.program_id(1)

    @pl.when(j == 0)
    def _():
        carry_ref[...] = jnp.zeros_like(carry_ref)

    if batch_rows:
        u3 = u_ref[...]
        u = jnp.swapaxes(u3, 0, 1).reshape(u3.shape[0] * u3.shape[1], u3.shape[2])
    else:
        u = u_ref[...]
    ub = u.astype(BF)
    y = None
    for c0 in range(0, S5_NSTATE, S5_LANES):
        re = slice(c0, c0 + S5_LANES)
        im = slice(S5_NSTATE + c0, S5_NSTATE + c0 + S5_LANES)
        z_ref[:, re] = _dot(ub, bbd_ref[:, re])
        z_ref[:, im] = _dot(ub, bbd_ref[:, im])
        steps = [(tab_ref[2 * i, :, re], tab_ref[2 * i + 1, :, re], 1 << i) for i in range(3)]
        pr, pi = tab_ref[6, :, re], tab_ref[7, :, re]
        cr, ci = carry_ref[:, re], carry_ref[:, im]
        last = 0 if rev else SUBLANES - 1
        if batch_rows:
            first = SUBLANES - 1 - last
            pr = jnp.broadcast_to(pr[first:first + 1, :], pr.shape)
            pi = jnp.broadcast_to(pi[first:first + 1, :], pi.shape)
        for i in range(ntile):
            t = (ntile - 1 - i) if rev else i
            rows = slice(t * SUBLANES, (t + 1) * SUBLANES)
            zr = z_ref[rows, re]
            zi = z_ref[rows, im]
            if not batch_rows:
                for ar, ai, sh in steps:
                    amt = SUBLANES - sh if rev else sh
                    sr = pltpu.roll(zr, amt, axis=0)
                    si = pltpu.roll(zi, amt, axis=0)
                    zr, zi = zr + ar * sr - ai * si, zi + ar * si + ai * sr
            zr, zi = zr + pr * cr - pi * ci, zi + pr * ci + pi * cr
            z_ref[rows, re] = zr
            z_ref[rows, im] = zi
            if batch_rows:
                cr, ci = zr, zi
            else:
                cr = jnp.broadcast_to(zr[last:last + 1, :], zr.shape)
                ci = jnp.broadcast_to(zi[last:last + 1, :], zi.shape)
        carry_ref[:, re] = cr
        carry_ref[:, im] = ci
        part = _dot(z_ref[:, re].astype(BF), cc_ref[re, :]) + _dot(z_ref[:, im].astype(BF), cc_ref[im, :])
        y = part if y is None else y + part
    if not rev:
        o_ref[...] = y
    else:
        y = y + yf_ref[...] + d_ref[...] * u
        hcur = 0.5 * y * (1.0 + jnp.tanh(math.sqrt(2.0 / math.pi) * (y + 0.044715 * (y * y * y))))
        gate = _dot(hcur.astype(BF), gw_ref[...]) + gb_ref[...]
        out = _rms(hcur * (1.0 / (1.0 + jnp.exp(-gate))), g_ref[...])
        if batch_rows:
            nseq, steps, width = o_ref.shape
            out = jnp.swapaxes(out.reshape(steps, nseq, width), 0, 1)
        o_ref[...] = out.astype(BF)


def _s5_tables(lam_re, lam_im, log_dt, b_re, b_im, c_re, c_im, rev):
    lre = jnp.minimum(lam_re.astype(F32), -1e-4)
    lim = lam_im.astype(F32)
    dt = jnp.exp(log_dt.astype(F32))[:, None]
    mag = jnp.exp(lre * dt)
    ar = mag * jnp.cos(lim * dt)
    ai = mag * jnp.sin(lim * dt)
    den = lre * lre + lim * lim
    nre = ar - 1.0
    coef_re = (nre * lre + ai * lim) / den
    coef_im = (ai * lre - nre * lim) / den
    bre, bim = b_re.astype(F32), b_im.astype(F32)
    bbar_re = coef_re[..., None] * bre - coef_im[..., None] * bim
    bbar_im = coef_re[..., None] * bim + coef_im[..., None] * bre
    eye = jnp.eye(S5_GROUPS, dtype=F32)
    bd = lambda m: jnp.einsum('gnc,gh->gchn', m, eye).reshape(S5_WIDTH, S5_NSTATE)
    bbd = jnp.concatenate([bd(bbar_re), bd(bbar_im)], axis=1).astype(BF)
    cd = lambda m: jnp.einsum('gcn,gh->gnhc', m.astype(F32), eye).reshape(S5_NSTATE, S5_WIDTH)
    cc = jnp.concatenate([cd(c_re), -cd(c_im)], axis=0).astype(BF)
    a1r, a1i = ar.reshape(1, -1), ai.reshape(1, -1)
    cmul = lambda xr, xi, yr, yi: (xr * yr - xi * yi, xr * yi + xi * yr)
    a2r, a2i = cmul(a1r, a1i, a1r, a1i)
    a4r, a4i = cmul(a2r, a2i, a2r, a2i)
    pw = [(a1r, a1i)]
    for _ in range(SUBLANES - 1):
        pw.append(cmul(pw[-1][0], pw[-1][1], a1r, a1i))
    pr = jnp.concatenate([x[0] for x in pw], axis=0)
    pi = jnp.concatenate([x[1] for x in pw], axis=0)
    rows = np.arange(SUBLANES)[:, None]
    tabs = []
    for (xr, xi), sh in (((a1r, a1i), 1), ((a2r, a2i), 2), ((a4r, a4i), 4)):
        keep = jnp.asarray(rows >= sh, F32)
        tabs += [keep * xr, keep * xi]
    tabs += [pr, pi]
    tab = jnp.stack(tabs)
    if rev:
        tab = tab[:, ::-1, :]
    return bbd, cc, tab


def _s5(u, tabs_f, tabs_b, d_skip, glu_w, glu_b, gain, L, batch_rows):
    B, S, _ = u.shape
    ntile = L // SUBLANES
    scratch = [pltpu.VMEM((L, 2 * S5_NSTATE), F32), pltpu.VMEM((SUBLANES, 2 * S5_NSTATE), F32)]
    if batch_rows:
        G, steps = B // SUBLANES, L // SUBLANES
        nch = S // steps
        io = lambda f: pl.BlockSpec((SUBLANES, steps, S5_WIDTH), lambda b, j: (b, f(j), 0))
    else:
        G, nch = B, S // L
        io = lambda f: pl.BlockSpec((None, L, S5_WIDTH), lambda b, j: (b, f(j), 0))
    part = lambda f: pl.BlockSpec((None, L, S5_WIDTH), lambda b, j: (b, f(j), 0))
    fwd, bwd = (lambda j: j), (lambda j: nch - 1 - j)
    kern = functools.partial(_s5_kernel, ntile=ntile, batch_rows=batch_rows)
    yf = pl.pallas_call(
        functools.partial(kern, rev=False), grid=(G, nch),
        in_specs=[io(fwd)] + [_const_spec(a.shape) for a in tabs_f],
        out_specs=part(fwd), out_shape=jax.ShapeDtypeStruct((G, nch * L, S5_WIDTH), F32),
        scratch_shapes=scratch, compiler_params=_cparams(("parallel", "arbitrary")), name="s5_fwd")(u, *tabs_f)
    rest = (d_skip, glu_w, glu_b, gain)
    return pl.pallas_call(
        functools.partial(kern, rev=True), grid=(G, nch),
        in_specs=[io(bwd), part(bwd)] + [_const_spec(a.shape) for a in tabs_b + rest],
        out_specs=io(bwd), out_shape=jax.ShapeDtypeStruct((B, S, S5_WIDTH), BF),
        scratch_shapes=scratch, compiler_params=_cparams(("parallel", "arbitrary")), name="s5_bwd")(
            u, yf, *tabs_b, *rest)


GQA_ACC_ROWS = HEAD_DIM + 16


def _gqa_kernel(q_ref, k_ref, vt_ref, g_ref, o_ref, s_ref, p_ref, m_ref, a_ref, a2_ref, acc_ref, *, tk, nkv):
    q = q_ref[...]
    tq = q.shape[0]
    halves = (q[:, :128], q[:, 128:])
    left = lax.broadcasted_iota(jnp.int32, (tq, 128), 1) < HEAD_DIM
    zero = jnp.zeros((tq, 128), BF)
    qs = [jnp.where(left, halves[0], zero), jnp.where(left, zero, halves[0]),
          jnp.where(left, halves[1], zero), jnp.where(left, zero, halves[1])]
    m_ref[...] = jnp.full(m_ref.shape, -jnp.inf, F32)
    acc_ref[...] = jnp.zeros(acc_ref.shape, F32)
    ones_rows = jnp.ones((GQA_ACC_ROWS - HEAD_DIM, tk), BF)

    def scores(jb, slot):
        k = k_ref[pl.ds(pl.multiple_of(jb * tk, tk), tk), :]
        for i in range(4):
            st = _dot_nt(k, qs[i])
            s_ref[slot, i] = st
            m_old = m_ref[i:i + 1, :]
            m_new = jnp.maximum(m_old, jnp.max(st, axis=0, keepdims=True))
            a_ref[i:i + 1, :] = jnp.exp2(m_old - m_new)
            m_ref[i:i + 1, :] = m_new

    def probs(slot):
        for i in range(4):
            p_ref[slot, i] = jnp.exp2(s_ref[slot, i] - m_ref[i:i + 1, :]).astype(BF)
            a2_ref[i:i + 1, :] = a_ref[i:i + 1, :]

    def values(jb, slot):
        vt = vt_ref[:, pl.ds(pl.multiple_of(jb * tk, tk), tk)]
        lhs = [jnp.concatenate([vt[g * HEAD_DIM:(g + 1) * HEAD_DIM, :], ones_rows], axis=0)
               for g in range(GQA_KV_HEADS)]
        for i in range(4):
            pv = _dot(lhs[i % 2], p_ref[slot, i])
            acc_ref[i] = a2_ref[i:i + 1, :] * acc_ref[i] + pv

    scores(0, 0)
    probs(0)
    scores(1, 1)

    def stage_triple(jb, slot):
        values(jb, slot)
        probs(1 - slot)
        scores(jb + 2, slot)

    def kv_body(jj, carry):
        stage_triple(2 * jj, 0)
        stage_triple(2 * jj + 1, 1)
        return carry

    lax.fori_loop(0, (nkv - 2) // 2, kv_body, 0)
    values(nkv - 2, (nkv - 2) % 2)
    probs((nkv - 1) % 2)
    values(nkv - 1, (nkv - 1) % 2)
    outs = [acc_ref[i, 0:HEAD_DIM, :] * (1.0 / acc_ref[i, HEAD_DIM:HEAD_DIM + 1, :]) for i in range(4)]
    ot = jnp.concatenate([outs[0], outs[2], outs[1], outs[3]], axis=0)
    o_ref[...] = _rms(ot.T, g_ref[...]).astype(BF)


def _gqa(qc, kc, vct, gain, tq, tk):
    B, S, _ = qc.shape
    assert S % (2 * tk) == 0 and S // tk >= 2, "the kv loop handles two blocks per trip"
    return pl.pallas_call(
        functools.partial(_gqa_kernel, tk=tk, nkv=S // tk), grid=(B, S // tq),
        in_specs=[pl.BlockSpec((None, tq, 256), lambda b, j: (b, j, 0)),
                  pl.BlockSpec((None, S, 128), lambda b, j: (b, 0, 0)),
                  pl.BlockSpec((None, 128, S), lambda b, j: (b, 0, 0)),
                  _const_spec(gain.shape)],
        out_specs=pl.BlockSpec((None, tq, MIXW), lambda b, j: (b, j, 0)),
        out_shape=jax.ShapeDtypeStruct((B, S, MIXW), BF),
        scratch_shapes=[pltpu.VMEM((2, 4, tk, tq), F32), pltpu.VMEM((2, 4, tk, tq), BF),
                        pltpu.VMEM((SUBLANES, tq), F32), pltpu.VMEM((SUBLANES, tq), F32),
                        pltpu.VMEM((SUBLANES, tq), F32), pltpu.VMEM((4, GQA_ACC_ROWS, tq), F32)],
        compiler_params=_cparams(("parallel", "parallel")), name="gqa")(qc, kc, vct, gain)


def _hgrn_level_masks(rev):
    t = np.arange(HG_CHUNK)[:, None]
    s = np.arange(HG_CHUNK)[None, :]
    out = []
    for m in HG_LEVELS:
        bt, bs = t // m, s // m
        ok = ((bt % 2 == 0) & (bs == bt + 1)) if rev else ((bt % 2 == 1) & (bs == bt - 1))
        out.append(np.tile(ok.astype(np.float32), (1, HGRN_HEADS)))
    for dlt in range(SUBLANES):
        ok = (s == (t + dlt if rev else t - dlt)) & (s // SUBLANES == t // SUBLANES)
        out.append(np.tile(ok.astype(np.float32), (1, HGRN_HEADS)))
    return np.stack(out)


def _hgrn_kernel(*refs, rev, nchunk, nseq):
    if rev:
        (q_ref, z_ref, v_ref, gate_ref, of_ref, lb_ref, tri_ref, ones_ref, lmask_ref, g_ref,
         o_ref, st_ref) = refs
    else:
        (q_ref, z_ref, v_ref, lb_ref, tri_ref, ones_ref, lmask_ref, o_ref, st_ref) = refs
    C = HG_CHUNK
    j = pl.program_id(1)

    @pl.when(j == 0)
    def _():
        st_ref[...] = jnp.zeros_like(st_ref)

    log_lb = lb_ref[0:1, :]
    log1m_lb = lb_ref[1:2, :]
    one_m_lb = lb_ref[2:3, :]
    tri = tri_ref[...]
    ones = ones_ref[...]
    row_in_tile = lax.broadcasted_iota(jnp.int32, (C, MIXW), 0) % SUBLANES
    bd_mask = (lax.broadcasted_iota(jnp.int32, (MIXW, MIXW), 0) // HEAD_DIM
               == lax.broadcasted_iota(jnp.int32, (MIXW, MIXW), 1) // HEAD_DIM)
    zero_row = jnp.zeros((1, MIXW), F32)

    def tile_roll(x, amt):
        return jnp.concatenate(
            [pltpu.roll(x[r:r + SUBLANES, :], amt, axis=0) for r in range(0, C, SUBLANES)], axis=0)

    def stack_heads(x):
        xb = x.astype(BF)
        return jnp.concatenate(
            [jnp.where(_head_mask(xb.shape, 1, h), xb, jnp.zeros_like(xb)) for h in range(HGRN_HEADS)], axis=0)

    def chunk_body(ci, carry):
        for si in range(nseq):
            seq_chunk(ci, si)
        return carry

    def seq_chunk(ci, si):
        c = (nchunk - 1 - ci) if rev else ci
        rows = pl.ds(pl.multiple_of(c * C, C), C)
        q = q_ref[si, rows, :]
        z = z_ref[si, rows, :]
        v = v_ref[si, rows, :]
        ls = jnp.minimum(z, 0.0) - jnp.log(1.0 + jnp.exp(-jnp.abs(z)))
        bv = log1m_lb + ls
        logf = jnp.maximum(log_lb, bv) + jnp.log(1.0 + jnp.exp(-jnp.abs(log_lb - bv)))
        kin = one_m_lb * jnp.exp(ls - z)
        pieces = _dot(tri, jnp.concatenate(_split3(logf * LOG2E), axis=1))
        b = pieces[:, 0:MIXW] + pieces[:, MIXW:2 * MIXW] + pieces[:, 2 * MIXW:3 * MIXW]
        edge = b[0:1, :] if rev else b[C - 1:C, :]

        st = st_ref[si]
        o = _dot_nt((q * jnp.exp2(b)).astype(BF), st.astype(BF))

        att = jnp.zeros((C, HGRN_HEADS * C), F32)
        for li, m in enumerate(HG_LEVELS):
            nb = C // m
            qrefs, krefs = [], []
            for blk in range(nb):
                if rev:
                    qrow = (blk + 1) * m
                    krow = blk * m
                else:
                    qrow = blk * m - 1
                    krow = (blk + 1) * m - 1
                qr = zero_row if (qrow < 0 or qrow >= C) else b[qrow:qrow + 1, :]
                qrefs.append(jnp.broadcast_to(qr, (m, MIXW)))
                krefs.append(jnp.broadcast_to(b[krow:krow + 1, :], (m, MIXW)))
            bq = jnp.concatenate(qrefs, axis=0)
            bk = jnp.concatenate(krefs, axis=0)
            ql = (q * jnp.exp2(b - bq)).astype(BF)
            kl = kin * jnp.exp2(bk - b)
            att = att + lmask_ref[li] * _dot_nt(ql, stack_heads(kl))
        ds = [(q * kin).astype(BF)]
        for dlt in range(1, SUBLANES):
            amt = SUBLANES - dlt if rev else dlt
            valid = (row_in_tile + dlt <= SUBLANES - 1) if rev else (row_in_tile >= dlt)
            ks = tile_roll(kin, amt)
            bs = tile_roll(b, amt)
            ds.append((q * ks * jnp.exp2(jnp.where(valid, b - bs, 0.0))).astype(BF))
        hsum = _dot(jnp.concatenate(ds, axis=0), ones)
        for dlt in range(SUBLANES):
            att = att + lmask_ref[len(HG_LEVELS) + dlt] * hsum[dlt * C:(dlt + 1) * C, :]
        o = o + _dot(att.astype(BF), stack_heads(v))

        kb = (kin * jnp.exp2(edge - b)).astype(BF)
        upd = _dot_tn(v.astype(BF), kb)
        st_ref[si] = st * jnp.exp2(edge) + jnp.where(bd_mask, upd, 0.0)

        if rev:
            tot = o + of_ref[si, rows, :]
            gate = gate_ref[si, rows, :]
            silu = gate * (1.0 / (1.0 + jnp.exp(-gate)))
            o_ref[si, rows, :] = (_rms(tot, g_ref[...]) * silu).astype(BF)
        else:
            o_ref[si, rows, :] = o

    lax.fori_loop(0, nchunk, chunk_body, 0, unroll=True)


def _hgrn(dd, lb_tab, gain, tb):
    B, S, _ = dd.shape
    ones = jnp.asarray(np.arange(MIXW)[:, None] // HEAD_DIM == np.arange(HGRN_HEADS * HG_CHUNK)[None, :] // HG_CHUNK,
                       BF)
    nblk = S // tb
    nchunk = tb // HG_CHUNK
    tri_f = jnp.asarray(np.tril(np.ones((HG_CHUNK, HG_CHUNK), np.float32)), BF)
    tri_b = jnp.asarray(np.triu(np.ones((HG_CHUNK, HG_CHUNK), np.float32)), BF)
    lm_f = jnp.asarray(_hgrn_level_masks(False))
    lm_b = jnp.asarray(_hgrn_level_masks(True))
    nseq = HG_SEQS
    assert B % nseq == 0
    fcol = lambda col: pl.BlockSpec((nseq, tb, MIXW), lambda b, j: (b, j, col))
    bcol = lambda col: pl.BlockSpec((nseq, tb, MIXW), lambda b, j: (b, nblk - 1 - j, col))
    scratch = [pltpu.VMEM((nseq, MIXW, MIXW), F32)]
    consts_f = (lb_tab, tri_f, ones, lm_f)
    of = pl.pallas_call(
        functools.partial(_hgrn_kernel, rev=False, nchunk=nchunk, nseq=nseq), grid=(B // nseq, nblk),
        in_specs=[fcol(0), fcol(1), fcol(3)] + [_const_spec(a.shape) for a in consts_f],
        out_specs=fcol(0), out_shape=jax.ShapeDtypeStruct((B, S, MIXW), F32),
        scratch_shapes=scratch, compiler_params=_cparams(("parallel", "arbitrary")), name="hgrn_fwd")(
            dd, dd, dd, *consts_f)
    consts_b = (lb_tab, tri_b, ones, lm_b, gain)
    return pl.pallas_call(
        functools.partial(_hgrn_kernel, rev=True, nchunk=nchunk, nseq=nseq), grid=(B // nseq, nblk),
        in_specs=[bcol(0), bcol(2), bcol(3), bcol(4), bcol(0)] + [_const_spec(a.shape) for a in consts_b],
        out_specs=bcol(0), out_shape=jax.ShapeDtypeStruct((B, S, MIXW), BF),
        scratch_shapes=scratch, compiler_params=_cparams(("parallel", "arbitrary")), name="hgrn_bwd")(
            dd, dd, dd, dd, of, *consts_b)


def _mem_kv_kernel(mem_ref, nw_ref, wkv_ref, o_ref):
    o_ref[...] = _dot(_rms(mem_ref[...], nw_ref[...]).astype(BF), wkv_ref[...]).astype(BF)


def _mem_kv(mem, nw, wkv):
    B, M, _ = mem.shape
    return pl.pallas_call(
        _mem_kv_kernel, grid=(B,),
        in_specs=[pl.BlockSpec((None, M, D_MODEL), lambda b: (b, 0, 0)), _const_spec(nw.shape),
                  _const_spec(wkv.shape)],
        out_specs=pl.BlockSpec((None, M, 2 * MIXW), lambda b: (b, 0, 0)),
        out_shape=jax.ShapeDtypeStruct((B, M, 2 * MIXW), BF),
        compiler_params=_cparams(("parallel",)), name="mem_kv")(mem, nw, wkv)


def _post_kernel(x_ref, ma_ref, mb_ref, mc_ref, md_ref, wo_ref, nx_ref, wq_ref, kv_ref, wxo_ref, o_ref):
    x1 = x_ref[...]
    for i, m_ref in enumerate((ma_ref, mb_ref, mc_ref, md_ref)):
        x1 = x1 + _dot(m_ref[...], wo_ref[i * MIXW:(i + 1) * MIXW, :])
    hq = _rms(x1, nx_ref[...]).astype(BF)
    q = _dot(hq, wq_ref[...]).astype(BF)
    kmem = kv_ref[:, :MIXW]
    vmem = kv_ref[:, MIXW:]
    acc = jnp.zeros(q.shape, F32)
    for h in range(XA_HEADS):
        qm = jnp.where(_head_mask(q.shape, 1, h), q, jnp.zeros_like(q))
        s = _dot_nt(qm, kmem)
        m = jnp.max(s, axis=-1, keepdims=True)
        e = jnp.exp(s - m)
        p = (e * (1.0 / jnp.sum(e, axis=-1, keepdims=True))).astype(BF)
        vm = jnp.where(_head_mask(vmem.shape, 1, h), vmem, jnp.zeros_like(vmem))
        acc = acc + _dot(p, vm)
    o_ref[...] = x1 + _dot(acc.astype(BF), wxo_ref[...])


def _post(x, ma, mb, mc, md, kv, p, tm):
    B, S, _ = x.shape
    tok = lambda w: pl.BlockSpec((None, tm, w), lambda b, j: (b, j, 0))
    consts = (p["w_out"], p["norm_xattn_w"], p["xq"])
    return pl.pallas_call(
        _post_kernel, grid=(B, S // tm),
        in_specs=[tok(D_MODEL), tok(MIXW), tok(MIXW), tok(MIXW), tok(MIXW)]
        + [_const_spec(a.shape) for a in consts]
        + [pl.BlockSpec((None, MEM_TOKENS, 2 * MIXW), lambda b, j: (b, 0, 0)), _const_spec(p["xo"].shape)],
        out_specs=tok(D_MODEL), out_shape=jax.ShapeDtypeStruct(x.shape, F32),
        compiler_params=_cparams(("parallel", "parallel")), name="post")(
            x, ma, mb, mc, md, *consts, kv, p["xo"])


def _ffn_kernel(xp_ref, xc_ref, xn_ref, nw_ref, wa_ref, wg_ref, cw_ref, cb_ref, wd_ref, fw_ref, o_ref,
                hn_ref, gx_ref, a_ref, act_ref, *, tm, final):
    j = pl.program_id(1)
    nj = pl.num_programs(1)
    nw = nw_ref[...]
    hp = _rms(xp_ref[...], nw)
    hnx = _rms(xn_ref[...], nw)
    hn_ref[0:HALO, :] = jnp.where(j > 0, hp, 0.0).astype(BF)
    hn_ref[HALO:HALO + tm, :] = _rms(xc_ref[...], nw).astype(BF)
    hn_ref[HALO + tm:2 * HALO + tm, :] = jnp.where(j < nj - 1, hnx, 0.0).astype(BF)

    def up(c, slot):
        a_ref[slot] = _dot(hn_ref[HALO:HALO + tm, :], wa_ref[c])
        gx_ref[slot] = _dot(hn_ref[...], wg_ref[c])

    def activate(c, slot):
        cw = cw_ref[c]
        gc = (cb_ref[c] + gx_ref[slot, HALO - 1:HALO - 1 + tm, :] * cw[0:1, :]
              + gx_ref[slot, HALO:HALO + tm, :] * cw[1:2, :]
              + gx_ref[slot, HALO + 1:HALO + 1 + tm, :] * cw[2:3, :])
        act_ref[:, c * FF_CHUNK:(c + 1) * FF_CHUNK] = (
            gc * (1.0 / (1.0 + jnp.exp(-gc))) * a_ref[slot]).astype(BF)

    up(0, 0)
    for c in range(1, FF_NCHUNK):
        up(c, c % 2)
        activate(c - 1, (c - 1) % 2)
    activate(FF_NCHUNK - 1, (FF_NCHUNK - 1) % 2)
    y = xc_ref[...] + _dot(act_ref[...], wd_ref[...])
    if final:
        y = _rms(y, fw_ref[...])
    o_ref[...] = y


def _ffn(x, p, final_w, tm, final):
    B, S, _ = x.shape
    per = tm // HALO
    nh = S // HALO
    consts = (p["norm_ffn_w"], p["up_a"], p["up_g"], p["conv_w"], p["conv_b"], p["down"], final_w)
    return pl.pallas_call(
        functools.partial(_ffn_kernel, tm=tm, final=final), grid=(B, S // tm),
        in_specs=[pl.BlockSpec((None, HALO, D_MODEL), lambda b, j: (b, jnp.maximum(j * per - 1, 0), 0)),
                  pl.BlockSpec((None, tm, D_MODEL), lambda b, j: (b, j, 0)),
                  pl.BlockSpec((None, HALO, D_MODEL), lambda b, j: (b, jnp.minimum((j + 1) * per, nh - 1), 0))]
        + [_const_spec(a.shape) for a in consts],
        out_specs=pl.BlockSpec((None, tm, D_MODEL), lambda b, j: (b, j, 0)),
        out_shape=jax.ShapeDtypeStruct(x.shape, F32),
        scratch_shapes=[pltpu.VMEM((tm + 2 * HALO, D_MODEL), BF), pltpu.VMEM((2, tm + 2 * HALO, FF_CHUNK), F32),
                        pltpu.VMEM((2, tm, FF_CHUNK), F32), pltpu.VMEM((tm, D_FF), BF)],
        compiler_params=_cparams(("parallel", "parallel")), name="ffn")(x, x, x, *consts)


def _rope_tables(S):
    t = jnp.arange(S)
    inv = 1.0 / (ROPE_THETA ** (jnp.arange(0, ROPE_AXIS_DIM, 2, dtype=F32) / ROPE_AXIS_DIM))
    ang = jnp.concatenate([(t // GRID_W).astype(F32)[:, None] * inv,
                           (t % GRID_W).astype(F32)[:, None] * inv], axis=-1)
    cos, sin = jnp.cos(ang), jnp.sin(ang)
    cos_h = jnp.concatenate([cos, cos], axis=-1)
    sin_h = jnp.concatenate([-sin, sin], axis=-1)
    return jnp.tile(cos_h, (1, GQA_Q_HEADS)), jnp.tile(sin_h, (1, GQA_Q_HEADS))


def _layer_params(l, w):
    row = lambda a: a.reshape(1, -1).astype(F32)
    w_in = w["w_in"][l]
    qscale = 1.0 / math.sqrt(HEAD_DIM)
    wa = jnp.concatenate([w_in[:, :256] * (qscale * LOG2E), w_in[:, 256:768]], axis=1)
    wq = w_in[:, 1024:1280].reshape(D_MODEL, GQA_Q_HEADS, HEAD_DIM)[:, np.array([0, 2, 1, 3]), :].reshape(
        D_MODEL, 256)
    blk = np.arange(256) // HEAD_DIM
    p = {
        "norm_mix_w": row(w["norm_mix_w"][l]),
        "wa": wa.astype(BF),
        "wb": w_in[:, 768:1024].astype(BF),
        "wq": wq.astype(BF),
        "wk": w_in[:, 1280:1408].astype(BF),
        "wvt": w_in[:, 1408:1536].T.astype(BF),
        "wd": w_in[:, 1536:].astype(BF),
        "ones": jnp.asarray(blk[:, None] == blk[None, :], BF),
        "gq": row(jnp.tile(w["gqa_q_norm_w"][l], GQA_Q_HEADS)),
        "gk": row(jnp.tile(w["gqa_k_norm_w"][l], GQA_Q_HEADS)),
        "na_bias": _na_bias_table(w["na_rpb"][l].astype(F32)),
        "s5_f": _s5_tables(*[w[k][l, 0] for k in ("s5_lambda_re", "s5_lambda_im", "s5_log_dt", "s5_b_re",
                                                    "s5_b_im", "s5_c_re", "s5_c_im")], rev=False),
        "s5_b": _s5_tables(*[w[k][l, 1] for k in ("s5_lambda_re", "s5_lambda_im", "s5_log_dt", "s5_b_re",
                                                    "s5_b_im", "s5_c_re", "s5_c_im")], rev=True),
        "s5_d": row(w["s5_d"][l]),
        "s5_glu_w": w["s5_glu_w"][l].astype(BF),
        "s5_glu_b": row(w["s5_glu_b"][l]),
        "w_out": w["w_out"][l].astype(BF),
        "norm_xattn_w": row(w["norm_xattn_w"][l]),
        "norm_mem_w": row(w["norm_mem_w"][l]),
        "xq": (w["xattn_w_q"][l] * qscale).astype(BF),
        "xkv": w["xattn_w_kv"][l].astype(BF),
        "xo": w["xattn_w_o"][l].astype(BF),
        "norm_ffn_w": row(w["norm_ffn_w"][l]),
        "up_a": w["ffn_w_up"][l][:, :D_FF].reshape(D_MODEL, FF_NCHUNK, FF_CHUNK).transpose(1, 0, 2).astype(BF),
        "up_g": w["ffn_w_up"][l][:, D_FF:].reshape(D_MODEL, FF_NCHUNK, FF_CHUNK).transpose(1, 0, 2).astype(BF),
        "conv_w": w["ffn_conv_w"][l].astype(F32).reshape(3, FF_NCHUNK, FF_CHUNK).transpose(1, 0, 2),
        "conv_b": w["ffn_conv_b"][l].astype(F32).reshape(FF_NCHUNK, 1, FF_CHUNK),
        "down": w["ffn_w_down"][l].astype(BF),
    }
    g = w["mix_out_norm_w"][l]
    p["g_a"], p["g_b"], p["g_c"], p["g_d"] = (row(g[i * MIXW:(i + 1) * MIXW]) for i in range(4))
    sm = jax.nn.softmax(w["hgrn_lower_bound"].astype(F32), axis=0)
    lb = jnp.concatenate([jnp.zeros_like(sm[:1]), jnp.cumsum(sm, axis=0)[:-1]], axis=0)[l]
    p["lb_tab"] = jnp.concatenate(
        [row(jnp.log(lb)), row(jnp.log1p(-lb)), row(1.0 - lb), jnp.zeros((SUBLANES - 3, MIXW), F32)], axis=0)
    return p


def _tiles(S):
    return {"tm": 512, "s5_l": 1024, "tq": 256, "tk": 512, "hg_tb": 512}


def _encode(x, mem, layers, final_w):
    B, S, _ = x.shape
    t = _tiles(S)
    cos, sin = _rope_tables(S)
    depth = len(layers)
    for l, p in enumerate(layers):
        pp = dict(p, cos=cos, sin=sin)
        qkva, ub, qc, kc, vct, dd = _in_proj(x, pp, t["tm"])
        ma = _na(qkva, p["na_bias"], p["g_a"])
        mb = _s5(ub, p["s5_f"], p["s5_b"], p["s5_d"], p["s5_glu_w"], p["s5_glu_b"], p["g_b"], t["s5_l"],
                 batch_rows=(B % SUBLANES == 0))
        mc = _gqa(qc, kc, vct, p["g_c"], t["tq"], t["tk"])
        md = _hgrn(dd, p["lb_tab"], p["g_d"], t["hg_tb"])
        kv = _mem_kv(mem, p["norm_mem_w"], p["xkv"])
        x = _post(x, ma, mb, mc, md, kv, p, t["tm"])
        x = _ffn(x, p, final_w, t["tm"], final=(l == depth - 1))
    return x


def kernel(x_prompt, x_sample, mem_prompt, mem_sample, norm_mix_w, w_in, na_rpb, s5_lambda_re, s5_lambda_im, s5_log_dt, s5_b_re, s5_b_im, s5_c_re, s5_c_im, s5_d, s5_glu_w, s5_glu_b, gqa_q_norm_w, gqa_k_norm_w, hgrn_lower_bound, mix_out_norm_w, w_out, norm_xattn_w, norm_mem_w, xattn_w_q, xattn_w_kv, xattn_w_o, norm_ffn_w, ffn_w_up, ffn_conv_w, ffn_conv_b, ffn_w_down, final_norm_w):
    w = dict(norm_mix_w=norm_mix_w, w_in=w_in, na_rpb=na_rpb, s5_lambda_re=s5_lambda_re,
             s5_lambda_im=s5_lambda_im, s5_log_dt=s5_log_dt, s5_b_re=s5_b_re, s5_b_im=s5_b_im,
             s5_c_re=s5_c_re, s5_c_im=s5_c_im, s5_d=s5_d, s5_glu_w=s5_glu_w, s5_glu_b=s5_glu_b,
             gqa_q_norm_w=gqa_q_norm_w, gqa_k_norm_w=gqa_k_norm_w, hgrn_lower_bound=hgrn_lower_bound,
             mix_out_norm_w=mix_out_norm_w, w_out=w_out, norm_xattn_w=norm_xattn_w, norm_mem_w=norm_mem_w,
             xattn_w_q=xattn_w_q, xattn_w_kv=xattn_w_kv, xattn_w_o=xattn_w_o, norm_ffn_w=norm_ffn_w,
             ffn_w_up=ffn_w_up, ffn_conv_w=ffn_conv_w, ffn_conv_b=ffn_conv_b, ffn_w_down=ffn_w_down)
    layers = [_layer_params(l, w) for l in range(w_in.shape[0])]
    final_w = final_norm_w.reshape(1, -1).astype(F32)
    return (_encode(x_prompt, mem_prompt, layers, final_w), _encode(x_sample, mem_sample, layers, final_w))
```
